```python
import math
import jax
import jax.numpy as jnp
from jax import lax
import numpy as np

D_MODEL = 4096
BATCH = 1
SEQ = 8192
DEPTH = 4

GRID_W = 64
CTX_LEN = 256
N_MIXERS = 3
N_SSD_LAYERS = (DEPTH + 2) // 3
N_S5_LAYERS = (DEPTH + 1) // 3
N_DIFF_LAYERS = DEPTH // 3
RMS_EPS = 1e-6

FFN_DIM = D_MODEL
FFN_CONV = 3

SSD_D_INNER = 2 * D_MODEL
SSD_HEAD_DIM = 64
SSD_HEADS = SSD_D_INNER // SSD_HEAD_DIM
SSD_GROUPS = 8
SSD_HEADS_PER_GROUP = SSD_HEADS // SSD_GROUPS
SSD_STATE = 128
SSD_GN = SSD_GROUPS * SSD_STATE
SSD_CONV = 5
SSD_CONV_CH = SSD_D_INNER + 2 * SSD_GN
SSD_IN_COLS = 2 * SSD_D_INNER + 2 * SSD_GN + 2 * SSD_HEADS
SSD_CHUNK = 128

S5_GROUP_CH = 16
S5_GROUPS = D_MODEL // S5_GROUP_CH
S5_STATE = 64
S5_BLOCK_GROUPS = 32
S5_N_BLOCKS = S5_GROUPS // S5_BLOCK_GROUPS

ATT_HEAD_DIM = 128
ATT_HEADS = D_MODEL // (2 * ATT_HEAD_DIM)
ATT_SCALE = ATT_HEAD_DIM ** -0.5
ATT_Q_BLOCK = 128
ROPE_BASE = 10000.0

kernel_name = "hybrid_ssd_s5_diffattn_prefix_dit"


def rmsnorm(x, gain):
    xf = x.astype(jnp.float32)
    y = xf * lax.rsqrt(jnp.mean(xf * xf, axis=-1, keepdims=True) + RMS_EPS)
    return (y * gain.astype(jnp.float32)).astype(x.dtype)


def group_rmsnorm(y, gain, groups):
    shp = y.shape
    yg = y.astype(jnp.float32).reshape(*shp[:-1], groups, shp[-1] // groups)
    yg = yg * lax.rsqrt(jnp.mean(yg * yg, axis=-1, keepdims=True) + RMS_EPS)
    return yg.reshape(shp) * gain.astype(jnp.float32)


def modulate(h, shift, scale):
    return h * (1.0 + scale) + shift


def orient(t, reverse):
    return t[:, ::-1] if reverse else t


def dwconv_centred(u, w, b):
    k = w.shape[0]
    r = k // 2
    n = u.shape[1]
    up = jnp.pad(u, ((0, 0), (r, r), (0, 0)))
    out = b + up[:, 0:n] * w[0]
    for j in range(1, k):
        out = out + up[:, j:j + n] * w[j]
    return out


def conv_ffn(u, w_up, conv_w, conv_b, w_down):
    a = dwconv_centred(u @ w_up, conv_w, conv_b)
    gate, val = jnp.split(a, 2, axis=-1)
    return (jax.nn.silu(gate) * val) @ w_down


def ssd_chunked_scan(xs, dt, a, bm, cm, h0):
    f32 = jnp.float32
    bsz, n = xs.shape[:2]
    nc = n // SSD_CHUNK
    xs, bm, cm = xs.astype(f32), bm.astype(f32), cm.astype(f32)

    def chunks(t):
        return t.reshape(bsz, nc, SSD_CHUNK, *t.shape[2:])

    xdt = chunks(xs * dt[..., None])
    bc, cc = chunks(bm), chunks(cm)
    da = jnp.moveaxis(chunks(dt * a), 2, -1)
    cum = jnp.cumsum(da, axis=-1)
    lower = jnp.tril(jnp.ones((SSD_CHUNK, SSD_CHUNK), dtype=bool))
    decay_in = jnp.exp(jnp.where(lower, cum[..., :, None] - cum[..., None, :], -jnp.inf))
    cb = jnp.einsum('bclgn,bcsgn->bcgls', cc, bc)
    y_diag = jnp.einsum('bcgkls,bcsgkp->bclgkp', cb[:, :, :, None] * decay_in, xdt)
    decay_to_end = jnp.exp(cum[..., -1:] - cum)
    chunk_states = jnp.einsum('bclgn,bcgkl,bclgkp->bcgkpn', bc, decay_to_end, xdt)
    chunk_decay = jnp.exp(cum[..., -1])

    def carry_state(h, inp):
        s, d = inp
        return h * d[..., None, None] + s, h

    h_last, h_enter = lax.scan(carry_state, h0.astype(f32),
                               (jnp.moveaxis(chunk_states, 1, 0), jnp.moveaxis(chunk_decay, 1, 0)))
    h_enter = jnp.moveaxis(h_enter, 0, 1)
    y_off = jnp.einsum('bclgn,bcgkpn->bclgkp', cc, h_enter) * jnp.moveaxis(jnp.exp(cum), -1, 2)[..., None]
    y = (y_diag + y_off).reshape(bsz, n, *xs.shape[2:])
    return y, h_last


def ssd_mixer(u_lat, u_ctx, w_in, conv_w, conv_b, a_log, dt_bias, d_skip, norm_w, w_out, want_ctx):
    f32 = jnp.float32
    a = -jnp.exp(a_log.astype(f32)).reshape(2, SSD_GROUPS, SSD_HEADS_PER_GROUP)
    dt_b = dt_bias.astype(f32).reshape(2, SSD_GROUPS, SSD_HEADS_PER_GROUP)
    d_h = d_skip.astype(f32).reshape(SSD_GROUPS, SSD_HEADS_PER_GROUP, 1)

    def project(u):
        bsz, n, _ = u.shape
        zxbcdt = u @ w_in
        z = zxbcdt[..., :SSD_D_INNER]
        xbc = jax.nn.silu(dwconv_centred(zxbcdt[..., SSD_D_INNER:SSD_D_INNER + SSD_CONV_CH], conv_w, conv_b))
        xs = xbc[..., :SSD_D_INNER].reshape(bsz, n, SSD_GROUPS, SSD_HEADS_PER_GROUP, SSD_HEAD_DIM)
        bm = xbc[..., SSD_D_INNER:SSD_D_INNER + SSD_GN].reshape(bsz, n, SSD_GROUPS, SSD_STATE)
        cm = xbc[..., SSD_D_INNER + SSD_GN:].reshape(bsz, n, SSD_GROUPS, SSD_STATE)
        dt_raw = zxbcdt[..., SSD_D_INNER + SSD_CONV_CH:].reshape(bsz, n, 2, SSD_GROUPS, SSD_HEADS_PER_GROUP)
        dt = jax.nn.softplus(dt_raw.astype(f32) + dt_b)
        return z, xs, bm, cm, dt

    def scan_both(xs, bm, cm, dt, h0_f, h0_b):
        y_f, hf = ssd_chunked_scan(xs, dt[:, :, 0], a[0], bm, cm, h0_f)
        y_b, hb = ssd_chunked_scan(xs[:, ::-1], dt[:, ::-1, 1], a[1], bm[:, ::-1], cm[:, ::-1], h0_b)
        return y_f + y_b[:, ::-1] + d_h * xs.astype(f32), hf, hb

    def finish(y, z):
        bsz, n = z.shape[:2]
        y = y.reshape(bsz, n, SSD_D_INNER) * jax.nn.silu(z.astype(f32))
        y = group_rmsnorm(y, norm_w, SSD_GROUPS)
        return y.astype(z.dtype) @ w_out

    zc, xc, bc, cc, dtc = project(u_ctx)
    zl, xl, bl, cl, dtl = project(u_lat)
    h0 = jnp.zeros((u_lat.shape[0], SSD_GROUPS, SSD_HEADS_PER_GROUP, SSD_HEAD_DIM, SSD_STATE), f32)
    yc, st_f, st_b = scan_both(xc, bc, cc, dtc, h0, h0)
    yl, _, _ = scan_both(xl, bl, cl, dtl, st_f, st_b)
    out_l = finish(yl, zl)
    out_c = finish(yc, zc) if want_ctx else None
    return out_l, out_c


def complex_affine_combine(e1, e2):
    a1r, a1i, b1r, b1i = e1
    a2r, a2i, b2r, b2i = e2
    return (a2r * a1r - a2i * a1i,
            a2r * a1i + a2i * a1r,
            a2r * b1r - a2i * b1i + b2r,
            a2r * b1i + a2i * b1r + b2i)


def diag_ssm_scan(u, ar, ai, bbr, bbi, cr, ci, h0r, h0i):
    uf = u.astype(jnp.float32)
    bur = jnp.einsum('btgc,gpc->btgp', uf, bbr)
    bui = jnp.einsum('btgc,gpc->btgp', uf, bbi)
    bur = bur.at[:, 0].add(ar * h0r - ai * h0i)
    bui = bui.at[:, 0].add(ar * h0i + ai * h0r)
    shape = bur.shape
    _, _, sr, si = lax.associative_scan(
        complex_affine_combine,
        (jnp.broadcast_to(ar, shape), jnp.broadcast_to(ai, shape), bur, bui), axis=1)
    y = jnp.einsum('gcp,btgp->btgc', cr, sr) - jnp.einsum('gcp,btgp->btgc', ci, si)
    return y, sr[:, -1], si[:, -1]


def s5_mixer(u_lat, u_ctx, lam_re, lam_im, log_step, b_re, b_im, c_re, c_im, d_skip, glu_w, glu_b, want_ctx):
    f32 = jnp.float32
    step = jnp.exp(log_step.astype(f32))[..., None]
    lr, li = lam_re.astype(f32), lam_im.astype(f32)
    mag = jnp.exp(lr * step)
    ar, ai = mag * jnp.cos(li * step), mag * jnp.sin(li * step)
    den = lr * lr + li * li
    kr = ((ar - 1.0) * lr + ai * li) / den
    ki = (ai * lr - (ar - 1.0) * li) / den
    br, bi = b_re.astype(f32), b_im.astype(f32)
    bbr = kr[..., None] * br - ki[..., None] * bi
    bbi = kr[..., None] * bi + ki[..., None] * br
    cr, ci = c_re.astype(f32), c_im.astype(f32)

    def u_blocks(u):
        bsz, n, _ = u.shape
        return jnp.moveaxis(u.reshape(bsz, n, S5_N_BLOCKS, S5_BLOCK_GROUPS, S5_GROUP_CH), 2, 0)

    def p_blocks(p):
        return jnp.moveaxis(p.reshape(2, S5_N_BLOCKS, S5_BLOCK_GROUPS, *p.shape[2:]), 1, 0)

    def run_block(args):
        ul, uc, ar_b, ai_b, bbr_b, bbi_b, cr_b, ci_b = args
        zero = jnp.zeros((ul.shape[0], S5_BLOCK_GROUPS, S5_STATE), f32)
        ys_l, ys_c = [], []
        for d in range(2):
            rev = d == 1
            pars = (ar_b[d], ai_b[d], bbr_b[d], bbi_b[d], cr_b[d], ci_b[d])
            yc_d, sr, si = diag_ssm_scan(orient(uc, rev), *pars, zero, zero)
            yl_d, _, _ = diag_ssm_scan(orient(ul, rev), *pars, sr, si)
            ys_l.append(orient(yl_d, rev))
            ys_c.append(orient(yc_d, rev))
        return ys_l[0] + ys_l[1], ys_c[0] + ys_c[1]

    yl_b, yc_b = lax.map(run_block, (u_blocks(u_lat), u_blocks(u_ctx), p_blocks(ar), p_blocks(ai),
                                     p_blocks(bbr), p_blocks(bbi), p_blocks(cr), p_blocks(ci)))

    def finish(yb, u):
        bsz, n, _ = u.shape
        y = jnp.moveaxis(yb, 0, 2).reshape(bsz, n, D_MODEL) + d_skip.astype(f32) * u.astype(f32)
        y = jax.nn.gelu(y).astype(u.dtype)
        z = y @ glu_w + glu_b
        return (z[..., :D_MODEL] * jax.nn.sigmoid(z[..., D_MODEL:])).astype(u.dtype)

    out_l = finish(yl_b, u_lat)
    out_c = finish(yc_b, u_ctx) if want_ctx else None
    return out_l, out_c


def axial_rope_tables(n):
    f32 = jnp.float32
    rows = n // GRID_W
    row = jnp.repeat(jnp.arange(rows, dtype=f32), GRID_W)
    col = jnp.tile(jnp.arange(GRID_W, dtype=f32), rows)
    n_freq = ATT_HEAD_DIM // 4
    inv_freq = ROPE_BASE ** (-jnp.arange(n_freq, dtype=f32) / n_freq)
    ang_r = row[:, None] * inv_freq
    ang_c = col[:, None] * inv_freq
    return jnp.cos(ang_r), jnp.sin(ang_r), jnp.cos(ang_c), jnp.sin(ang_c)


def rotate_pairs(x, cos, sin):
    x1, x2 = jnp.split(x, 2, axis=-1)
    cos, sin = cos[None, :, None, :], sin[None, :, None, :]
    return jnp.concatenate([x1 * cos - x2 * sin, x1 * sin + x2 * cos], axis=-1)


def apply_axial_rope(x, cos_r, sin_r, cos_c, sin_c):
    xr, xc = jnp.split(x, 2, axis=-1)
    return jnp.concatenate([rotate_pairs(xr, cos_r, sin_r), rotate_pairs(xc, cos_c, sin_c)], axis=-1).astype(x.dtype)


def diff_softmax_attend(q, k, v, lam):
    bsz, nq = q.shape[:2]
    nk = k.shape[1]
    s = jnp.einsum('bqhd,bkhd->bhqk', q, k).astype(jnp.float32) * ATT_SCALE
    p = jax.nn.softmax(s, axis=-1).reshape(bsz, ATT_HEADS, 2, nq, nk)
    a = p[:, :, 0] - lam * p[:, :, 1]
    return jnp.einsum('bhqk,bkhe->bqhe', a, v.astype(jnp.float32))


def diff_attn_mixer(u_lat, u_ctx, w_q, w_k, w_v, w_o, q_gain, k_gain, lam_q1, lam_k1, lam_q2, lam_k2,
                    sub_gain, lambda_init, want_ctx):
    f32 = jnp.float32
    bsz, seq, _ = u_lat.shape

    def qkv(u):
        n = u.shape[1]
        q = rmsnorm((u @ w_q).reshape(bsz, n, 2 * ATT_HEADS, ATT_HEAD_DIM), q_gain)
        k = rmsnorm((u @ w_k).reshape(bsz, n, 2 * ATT_HEADS, ATT_HEAD_DIM), k_gain)
        v = (u @ w_v).reshape(bsz, n, ATT_HEADS, 2 * ATT_HEAD_DIM)
        return q, k, v

    ql, kl, vl = qkv(u_lat)
    qc, kc, vc = qkv(u_ctx)
    rope = axial_rope_tables(seq)
    ql, kl = apply_axial_rope(ql, *rope), apply_axial_rope(kl, *rope)
    lam = (jnp.exp(jnp.sum(lam_q1.astype(f32) * lam_k1.astype(f32)))
           - jnp.exp(jnp.sum(lam_q2.astype(f32) * lam_k2.astype(f32))) + lambda_init)

    k_all = jnp.concatenate([kc, kl], axis=1)
    v_all = jnp.concatenate([vc, vl], axis=1)
    nb = seq // ATT_Q_BLOCK
    q_blocks = jnp.moveaxis(ql.reshape(bsz, nb, ATT_Q_BLOCK, 2 * ATT_HEADS, ATT_HEAD_DIM), 1, 0)
    o_l = lax.map(lambda qb: diff_softmax_attend(qb, k_all, v_all, lam), q_blocks)
    o_l = jnp.moveaxis(o_l, 0, 1).reshape(bsz, seq, ATT_HEADS, 2 * ATT_HEAD_DIM)

    def finish(o):
        o = rmsnorm(o, sub_gain) * (1.0 - lambda_init)
        return o.reshape(bsz, o.shape[1], D_MODEL).astype(u_lat.dtype) @ w_o

    out_l = finish(o_l)
    out_c = finish(diff_softmax_attend(qc, kc, vc, lam)) if want_ctx else None
    return out_l, out_c


def setup_inputs(seed: int = 0) -> dict:
    key = jax.random.key(seed)
    keys = jax.random.split(key, 48)
    counter = [0]
    f32 = jnp.float32

    def nk():
        k = keys[counter[0]]
        counter[0] += 1
        return k

    def normal(shape, scale):
        return scale * jax.random.normal(nk(), shape, f32)

    def gain(shape):
        return 1.0 + 0.1 * jax.random.normal(nk(), shape, f32)

    def log_uniform(shape, lo, hi):
        return jax.random.uniform(nk(), shape, f32, math.log(lo), math.log(hi))

    D, F = D_MODEL, FFN_DIM
    inp = {}
    inp['x'] = normal((BATCH, SEQ, D), 1.0)
    inp['c'] = normal((BATCH, D), 1.0)
    inp['ctx'] = normal((BATCH, CTX_LEN, D), 1.0)
    inp['c_ctx'] = normal((D,), 1.0)
    inp['mod_w'] = normal((DEPTH, D, 6 * D), 0.5 * D ** -0.5)
    inp['mod_b'] = normal((DEPTH, 6 * D), 0.02)
    inp['norm_mix'] = gain((DEPTH, D))
    inp['norm_ffn'] = gain((DEPTH, D))
    inp['ffn_up'] = normal((DEPTH, D, 2 * F), D ** -0.5)
    inp['ffn_conv_w'] = normal((DEPTH, FFN_CONV, 2 * F), FFN_CONV ** -0.5)
    inp['ffn_conv_b'] = normal((DEPTH, 2 * F), 0.02)
    inp['ffn_down'] = normal((DEPTH, F, D), F ** -0.5)
    na = N_SSD_LAYERS
    inp['ssd_w_in'] = normal((na, D, SSD_IN_COLS), D ** -0.5)
    inp['ssd_conv_w'] = normal((na, SSD_CONV, SSD_CONV_CH), SSD_CONV ** -0.5)
    inp['ssd_conv_b'] = normal((na, SSD_CONV_CH), 0.02)
    inp['ssd_a_log'] = jnp.log(jax.random.uniform(nk(), (na, 2, SSD_HEADS), f32, 1.0, 16.0))
    dt0 = jnp.exp(log_uniform((na, 2, SSD_HEADS), 0.001, 0.1))
    inp['ssd_dt_bias'] = dt0 + jnp.log(-jnp.expm1(-dt0))
    inp['ssd_d'] = gain((na, SSD_HEADS))
    inp['ssd_norm'] = gain((na, SSD_D_INNER))
    inp['ssd_w_out'] = normal((na, SSD_D_INNER, D), SSD_D_INNER ** -0.5)
    nb = N_S5_LAYERS
    inp['s5_lam_re'] = -0.5 + normal((nb, 2, S5_GROUPS, S5_STATE), 0.01)
    inp['s5_lam_im'] = (jnp.broadcast_to(math.pi * jnp.arange(S5_STATE, dtype=f32), (nb, 2, S5_GROUPS, S5_STATE))
                        + normal((nb, 2, S5_GROUPS, S5_STATE), 0.01))
    inp['s5_log_step'] = log_uniform((nb, 2, S5_GROUPS), 0.001, 0.1)
    inp['s5_b_re'] = normal((nb, S5_GROUPS, S5_STATE, S5_GROUP_CH), (2 * S5_GROUP_CH) ** -0.5)
    inp['s5_b_im'] = normal((nb, S5_GROUPS, S5_STATE, S5_GROUP_CH), (2 * S5_GROUP_CH) ** -0.5)
    inp['s5_c_re'] = normal((nb, 2, S5_GROUPS, S5_GROUP_CH, S5_STATE), S5_STATE ** -0.5)
    inp['s5_c_im'] = normal((nb, 2, S5_GROUPS, S5_GROUP_CH, S5_STATE), S5_STATE ** -0.5)
    inp['s5_d'] = gain((nb, D))
    inp['s5_glu_w'] = normal((nb, D, 2 * D), D ** -0.5)
    inp['s5_glu_b'] = normal((nb, 2 * D), 0.02)
    nc = N_DIFF_LAYERS
    inp['da_w_q'] = normal((nc, D, D), D ** -0.5)
    inp['da_w_k'] = normal((nc, D, D), D ** -0.5)
    inp['da_w_v'] = normal((nc, D, D), D ** -0.5)
    inp['da_w_o'] = normal((nc, D, D), D ** -0.5)
    inp['da_q_norm'] = gain((nc, ATT_HEAD_DIM))
    inp['da_k_norm'] = gain((nc, ATT_HEAD_DIM))
    inp['da_lam_q1'] = normal((nc, ATT_HEAD_DIM), 0.1)
    inp['da_lam_k1'] = normal((nc, ATT_HEAD_DIM), 0.1)
    inp['da_lam_q2'] = normal((nc, ATT_HEAD_DIM), 0.1)
    inp['da_lam_k2'] = normal((nc, ATT_HEAD_DIM), 0.1)
    inp['da_sub_norm'] = gain((nc, 2 * ATT_HEAD_DIM))
    return inp


def reference(x, c, ctx, c_ctx, mod_w, mod_b, norm_mix, norm_ffn, ffn_up, ffn_conv_w, ffn_conv_b, ffn_down,
              ssd_w_in, ssd_conv_w, ssd_conv_b, ssd_a_log, ssd_dt_bias, ssd_d, ssd_norm, ssd_w_out,
              s5_lam_re, s5_lam_im, s5_log_step, s5_b_re, s5_b_im, s5_c_re, s5_c_im, s5_d, s5_glu_w, s5_glu_b,
              da_w_q, da_w_k, da_w_v, da_w_o, da_q_norm, da_k_norm, da_lam_q1, da_lam_k1, da_lam_q2, da_lam_k2,
              da_sub_norm):
    cond_lat = jax.nn.silu(c)
    cond_ctx = jax.nn.silu(c_ctx)
    for i in range(DEPTH):
        want_ctx = i < DEPTH - 1
        sh1, sc1, g1, sh2, sc2, g2 = jnp.split((cond_lat @ mod_w[i] + mod_b[i])[:, None, :], 6, axis=-1)
        csh1, csc1, cg1, csh2, csc2, cg2 = jnp.split((cond_ctx @ mod_w[i] + mod_b[i])[None, None, :], 6, axis=-1)
        h = modulate(rmsnorm(x, norm_mix[i]), sh1, sc1)
        hc = modulate(rmsnorm(ctx, norm_mix[i]), csh1, csc1)
        kind, j = i % N_MIXERS, i // N_MIXERS
        if kind == 0:
            o, oc = ssd_mixer(h, hc, ssd_w_in[j], ssd_conv_w[j], ssd_conv_b[j], ssd_a_log[j], ssd_dt_bias[j],
                              ssd_d[j], ssd_norm[j], ssd_w_out[j], want_ctx)
        elif kind == 1:
            o, oc = s5_mixer(h, hc, s5_lam_re[j], s5_lam_im[j], s5_log_step[j], s5_b_re[j], s5_b_im[j],
                             s5_c_re[j], s5_c_im[j], s5_d[j], s5_glu_w[j], s5_glu_b[j], want_ctx)
        else:
            lambda_init = 0.8 - 0.6 * math.exp(-0.3 * i)
            o, oc = diff_attn_mixer(h, hc, da_w_q[j], da_w_k[j], da_w_v[j], da_w_o[j], da_q_norm[j], da_k_norm[j],
                                    da_lam_q1[j], da_lam_k1[j], da_lam_q2[j], da_lam_k2[j], da_sub_norm[j],
                                    lambda_init, want_ctx)
        x = x + g1 * o
        hf = modulate(rmsnorm(x, norm_ffn[i]), sh2, sc2)
        x = x + g2 * conv_ffn(hf, ffn_up[i], ffn_conv_w[i], ffn_conv_b[i], ffn_down[i])
        if want_ctx:
            ctx = ctx + cg1 * oc
            hfc = modulate(rmsnorm(ctx, norm_ffn[i]), csh2, csc2)
            ctx = ctx + cg2 * conv_ffn(hfc, ffn_up[i], ffn_conv_w[i], ffn_conv_b[i], ffn_down[i])
    return x
```

```python
import functools
import math

import jax
import jax.numpy as jnp
from jax import lax
from jax.experimental import pallas as pl
from jax.experimental.pallas import tpu as pltpu

F32 = jnp.float32
BF16 = jnp.bfloat16

RMS_EPS = 1e-6
ROPE_BASE = 10000.0
GRID_W = 64
N_MIXERS = 3

LANES = 128
SUBLANES = 8
VMEM_LIMIT_BYTES = 52 * 1024 * 1024

SSD_GROUPS = 8
SSD_HEAD_DIM = 64
SSD_STATE = 128
SSD_CHUNK = 128
S5_SEGMENTS = SUBLANES
S5_TIME_TILE = 32


def _params(*semantics):
    return pltpu.CompilerParams(dimension_semantics=semantics, vmem_limit_bytes=VMEM_LIMIT_BYTES)


def _pick(n, candidates):
    for c in candidates:
        if n % c == 0:
            return c
    raise ValueError(f"no tile in {candidates} divides {n}")


def _sigmoid(x):
    return 1.0 / (1.0 + jnp.exp(-x))


def _silu(x):
    return x * _sigmoid(x)


def _mod_kernel(c_ref, w_ref, b_ref, o_ref, acc_ref, *, nk):
    k = pl.program_id(2)

    @pl.when(k == 0)
    def _():
        acc_ref[...] = jnp.zeros_like(acc_ref)

    w = w_ref[0]
    tk, tn = w.shape
    for r in range(2):
        cb = jnp.tile(c_ref[r], (1, tn // LANES))
        acc_ref[r] += (w * cb).reshape(tk // SUBLANES, SUBLANES, tn).sum(axis=0)

    @pl.when(k == nk - 1)
    def _():
        o_ref[0] = acc_ref[...].sum(axis=1) + b_ref[0]


def _modulation(cond, mod_w, mod_b):
    depth, d, n = mod_w.shape
    tk = _pick(d, (512, 256, 128))
    tn = _pick(n, (2048, 1024, 512))
    nk = d // tk
    condb = jnp.broadcast_to(cond[:, :, None], (2, d, LANES))
    return pl.pallas_call(
        functools.partial(_mod_kernel, nk=nk),
        grid=(depth, n // tn, nk),
        in_specs=[
            pl.BlockSpec((2, tk, LANES), lambda l, j, k: (0, k, 0)),
            pl.BlockSpec((1, tk, tn), lambda l, j, k: (l, k, j)),
            pl.BlockSpec((1, 1, tn), lambda l, j, k: (l, 0, j)),
        ],
        out_specs=pl.BlockSpec((1, 2, tn), lambda l, j, k: (l, 0, j)),
        out_shape=jax.ShapeDtypeStruct((depth, 2, n), F32),
        scratch_shapes=[pltpu.VMEM((2, SUBLANES, tn), F32)],
        compiler_params=_params("parallel", "parallel", "arbitrary"),
    )(condb, mod_w, mod_b.reshape(depth, 1, n))


def _norm_mod_kernel(x_ref, g_ref, m_ref, o_ref, *, shift_row, scale_row):
    x = x_ref[...]
    y = x * lax.rsqrt(jnp.mean(x * x, axis=-1, keepdims=True) + RMS_EPS) * g_ref[...]
    m = m_ref[0]
    o_ref[...] = (y * (1.0 + m[scale_row:scale_row + 1]) + m[shift_row:shift_row + 1]).astype(o_ref.dtype)


def _norm_mod(xa, gain, mod, shift_row, scale_row, ctx_len, out_dtype):
    t, d = xa.shape
    tm = _pick(ctx_len, (256, 128))
    ncb = ctx_len // tm
    return pl.pallas_call(
        functools.partial(_norm_mod_kernel, shift_row=shift_row, scale_row=scale_row),
        grid=(t // tm,),
        in_specs=[
            pl.BlockSpec((tm, d), lambda i: (i, 0)),
            pl.BlockSpec((1, d), lambda i: (0, 0)),
            pl.BlockSpec((1, 6, d), lambda i: (jnp.where(i < ncb, 1, 0), 0, 0)),
        ],
        out_specs=pl.BlockSpec((tm, d), lambda i: (i, 0)),
        out_shape=jax.ShapeDtypeStruct((t, d), out_dtype),
        compiler_params=_params("parallel"),
    )(xa, gain.reshape(1, d), mod)


def _gate_rows(gate_ref, i, tm, tn, ctx_len):
    rows = i * tm + lax.broadcasted_iota(jnp.int32, (tm, tn), 0)
    return jnp.where(rows < ctx_len, gate_ref[1:2, :], gate_ref[0:1, :])


def _mm_kernel(*refs, nk, mode, ctx_len):
    i = pl.program_id(1)
    k = pl.program_id(2)
    if mode == "plain":
        a_ref, w_ref, o_ref = refs[:3]
        accs = refs[3:]
        ws = (w_ref,)
    elif mode == "res":
        a_ref, w_ref, res_ref, gate_ref, o_ref = refs[:5]
        accs = refs[5:]
        ws = (w_ref,)
    else:
        a_ref, wa_ref, wb_ref, ba_ref, bb_ref, res_ref, gate_ref, o_ref = refs[:8]
        accs = refs[8:]
        ws = (wa_ref, wb_ref)

    a = a_ref[...]
    prods = [jnp.dot(a, w[...], preferred_element_type=F32) for w in ws]

    def finish(vals):
        tm, tn = vals[0].shape
        if mode == "plain":
            o_ref[...] = vals[0].astype(o_ref.dtype)
        elif mode == "res":
            o_ref[...] = res_ref[...] + _gate_rows(gate_ref, i, tm, tn, ctx_len) * vals[0]
        else:
            o = (vals[0] + ba_ref[...]) * _sigmoid(vals[1] + bb_ref[...])
            o_ref[...] = res_ref[...] + _gate_rows(gate_ref, i, tm, tn, ctx_len) * o

    if nk == 1:
        finish(prods)
    else:
        @pl.when(k == 0)
        def _():
            for acc, p in zip(accs, prods):
                acc[...] = p

        @pl.when(k > 0)
        def _():
            for acc, p in zip(accs, prods):
                acc[...] += p

        @pl.when(k == nk - 1)
        def _():
            finish([acc[...] for acc in accs])


def _matmul(a, w, *, out_dtype=F32, mode="plain", res=None, gate=None, bias=None, ctx_len=0):
    m, kdim = a.shape
    n_w = w.shape[1]
    n = n_w // 2 if mode == "glu" else n_w
    tm = _pick(m, (768, 512, 256))
    tn = n if n < LANES else _pick(n, (512, 256, 128))
    tk = _pick(kdim, (4096, 2048, 1024, 512))
    nk = kdim // tk
    nb = n // tn
    a_spec = pl.BlockSpec((tm, tk), lambda j, i, k: (i, k))
    o_spec = pl.BlockSpec((tm, tn), lambda j, i, k: (i, j))
    w_spec = pl.BlockSpec((tk, tn), lambda j, i, k: (k, j))
    row_spec = pl.BlockSpec((1, tn), lambda j, i, k: (0, j))
    gate_spec = pl.BlockSpec((2, tn), lambda j, i, k: (0, j))
    if mode == "plain":
        in_specs, args, nacc = [a_spec, w_spec], (a, w), 1
    elif mode == "res":
        in_specs, args, nacc = [a_spec, w_spec, o_spec, gate_spec], (a, w, res, gate), 1
    else:
        wb_spec = pl.BlockSpec((tk, tn), lambda j, i, k: (k, j + nb))
        bb_spec = pl.BlockSpec((1, tn), lambda j, i, k: (0, j + nb))
        b2 = bias.reshape(1, n_w)
        in_specs = [a_spec, w_spec, wb_spec, row_spec, bb_spec, o_spec, gate_spec]
        args, nacc = (a, w, w, b2, b2, res, gate), 2
    scratch = [pltpu.VMEM((tm, tn), F32)] * nacc if nk > 1 else []
    return pl.pallas_call(
        functools.partial(_mm_kernel, nk=nk, mode=mode, ctx_len=ctx_len),
        grid=(nb, m // tm, nk),
        in_specs=in_specs,
        out_specs=o_spec,
        out_shape=jax.ShapeDtypeStruct((m, n), out_dtype),
        scratch_shapes=scratch,
        compiler_params=_params("parallel", "parallel", "arbitrary"),
    )(*args)


def _conv_window(cur_ref, prev_ref, next_ref, w_ref, b_ref, i, tm, ctx_len, total):
    taps = w_ref.shape[0]
    r = taps // 2
    start = i * tm
    at_start = jnp.logical_or(start == 0, start == ctx_len)
    at_end = jnp.logical_or(start + tm == ctx_len, start + tm == total)
    cur = cur_ref[...].astype(F32)
    prev = jnp.where(at_start, 0.0, prev_ref[...].astype(F32))
    nxt = jnp.where(at_end, 0.0, next_ref[...].astype(F32))
    ext = jnp.concatenate([prev, cur, nxt], axis=0)
    out = b_ref[...] + ext[SUBLANES - r:SUBLANES - r + tm] * w_ref[0:1, :]
    for j in range(1, taps):
        off = SUBLANES - r + j
        out = out + ext[off:off + tm] * w_ref[j:j + 1, :]
    return out


def _conv_silu_kernel(cur_ref, prev_ref, next_ref, w_ref, b_ref, o_ref, *, tm, ctx_len, total):
    i = pl.program_id(0)
    o_ref[...] = _silu(_conv_window(cur_ref, prev_ref, next_ref, w_ref, b_ref, i, tm, ctx_len, total)
                       ).astype(o_ref.dtype)


def _conv_glu_kernel(gc_ref, gp_ref, gn_ref, vc_ref, vp_ref, vn_ref, wg_ref, wv_ref, bg_ref, bv_ref, o_ref,
                     *, tm, ctx_len, total):
    i = pl.program_id(0)
    g = _conv_window(gc_ref, gp_ref, gn_ref, wg_ref, bg_ref, i, tm, ctx_len, total)
    v = _conv_window(vc_ref, vp_ref, vn_ref, wv_ref, bv_ref, i, tm, ctx_len, total)
    o_ref[...] = (_silu(g) * v).astype(o_ref.dtype)


def _conv_specs(tm, tc, taps, nrow8, col_off):
    hb = tm // SUBLANES
    cur = pl.BlockSpec((tm, tc), lambda i, j: (i, j + col_off))
    prev = pl.BlockSpec((SUBLANES, tc), lambda i, j: (jnp.maximum(i * hb - 1, 0), j + col_off))
    nxt = pl.BlockSpec((SUBLANES, tc), lambda i, j: (jnp.minimum((i + 1) * hb, nrow8 - 1), j + col_off))
    w = pl.BlockSpec((taps, tc), lambda i, j: (0, j + col_off))
    b = pl.BlockSpec((1, tc), lambda i, j: (0, j + col_off))
    return cur, prev, nxt, w, b


def _conv_silu(a, w, b, ctx_len, out_dtype):
    t, c = a.shape
    taps = w.shape[0]
    tm = _pick(ctx_len, (256, 128))
    tc = _pick(c, (1024, 512, 256, 128))
    cur, prev, nxt, ws, bs = _conv_specs(tm, tc, taps, t // SUBLANES, 0)
    return pl.pallas_call(
        functools.partial(_conv_silu_kernel, tm=tm, ctx_len=ctx_len, total=t),
        grid=(t // tm, c // tc),
        in_specs=[cur, prev, nxt, ws, bs],
        out_specs=pl.BlockSpec((tm, tc), lambda i, j: (i, j)),
        out_shape=jax.ShapeDtypeStruct((t, c), out_dtype),
        compiler_params=_params("parallel", "parallel"),
    )(a, a, a, w, b.reshape(1, c))


def _conv_glu(a, w, b, ctx_len, out_dtype):
    t, c2 = a.shape
    f = c2 // 2
    taps = w.shape[0]
    tm = _pick(ctx_len, (256, 128))
    tc = _pick(f, (1024, 512, 256, 128))
    g = _conv_specs(tm, tc, taps, t // SUBLANES, 0)
    v = _conv_specs(tm, tc, taps, t // SUBLANES, f // tc)
    b2 = b.reshape(1, c2)
    return pl.pallas_call(
        functools.partial(_conv_glu_kernel, tm=tm, ctx_len=ctx_len, total=t),
        grid=(t // tm, f // tc),
        in_specs=[g[0], g[1], g[2], v[0], v[1], v[2], g[3], v[3], g[4], v[4]],
        out_specs=pl.BlockSpec((tm, tc), lambda i, j: (i, j)),
        out_shape=jax.ShapeDtypeStruct((t, f), out_dtype),
        compiler_params=_params("parallel", "parallel"),
    )(a, a, a, a, a, a, w, w, b2, b2)


def _split3(v):
    h1 = v.astype(BF16)
    r1 = v - h1.astype(F32)
    h2 = r1.astype(BF16)
    h3 = (r1 - h2.astype(F32)).astype(BF16)
    return h1, h2, h3


def _ssd_dt_kernel(raw_ref, bias_ref, a_ref, dt_ref, cum_ref, tot_ref, *, nfwd):
    x = raw_ref[...] + bias_ref[...]
    dt = jnp.maximum(x, 0.0) + jnp.log(1.0 + jnp.exp(-jnp.abs(x)))
    dt_ref[...] = dt
    da = dt * a_ref[...]
    l = da.shape[0]
    row = lax.broadcasted_iota(jnp.int32, (l, l), 0)
    col = lax.broadcasted_iota(jnp.int32, (l, l), 1)
    incl_m = jnp.where(row >= col, 1.0, 0.0).astype(BF16)
    excl_m = jnp.where(row > col, 1.0, 0.0).astype(BF16)
    parts = _split3(da)
    incl = sum(jnp.dot(incl_m, p, preferred_element_type=F32) for p in parts)
    excl = sum(jnp.dot(excl_m, p, preferred_element_type=F32) for p in parts)
    lane = lax.broadcasted_iota(jnp.int32, da.shape, 1)
    cum_ref[...] = jnp.where(lane < nfwd, incl, excl)
    tot_ref[0] = incl[l - 1:l, :]


def _ssd_dt(dt_raw, dt_bias, a_neg):
    t, c = dt_raw.shape
    nc = t // SSD_CHUNK
    row = pl.BlockSpec((1, c), lambda i: (0, 0))
    blk = pl.BlockSpec((SSD_CHUNK, c), lambda i: (i, 0))
    return pl.pallas_call(
        functools.partial(_ssd_dt_kernel, nfwd=c // 2),
        grid=(nc,),
        in_specs=[blk, row, row],
        out_specs=[blk, blk, pl.BlockSpec((1, 1, c), lambda i: (i, 0, 0))],
        out_shape=[jax.ShapeDtypeStruct((t, c), F32), jax.ShapeDtypeStruct((t, c), F32),
                   jax.ShapeDtypeStruct((nc, 1, c), F32)],
        compiler_params=_params("parallel"),
    )(dt_raw, dt_bias.reshape(1, c), a_neg.reshape(1, c))


def _ssd_scan_kernel(xf_ref, bf_ref, cf_ref, pf_ref, ptf_ref, totf_ref, tef_ref,
                     xb_ref, bb_ref, cb_ref, pb_ref, totb_ref, teb_ref,
                     e_ref, dsk_ref, yf_ref, yb_ref, h_ref, *, hpg):
    pdim = SSD_HEAD_DIM
    j = pl.program_id(1)

    @pl.when(j == 0)
    def _():
        h_ref[...] = jnp.zeros_like(h_ref)

    e = e_ref[...]

    def expand(v):
        hi = v.astype(BF16)
        lo = (v - hi.astype(F32)).astype(BF16)
        return jnp.dot(hi, e, preferred_element_type=F32) + jnp.dot(lo, e, preferred_element_type=F32)

    def state_update(d, x32, bm, scale, total_decay):
        xw = (x32 * expand(scale)).astype(BF16)
        inc = lax.dot_general(bm, xw, (((0,), (0,)), ((), ())), preferred_element_type=F32)
        h_ref[d] = h_ref[d] * total_decay + inc

    x = xf_ref[...]
    x32 = x.astype(F32)
    bm = bf_ref[...]
    cm = cf_ref[...]
    p = pf_ref[0]
    pt = ptf_ref[0]
    dtf, cumf = p[:, 0:hpg], p[:, hpg:2 * hpg]
    cumb = p[:, 3 * hpg:4 * hpg]
    totf = totf_ref[0, 0][:, 0:hpg]
    l = x.shape[0]
    cb = lax.dot_general(cm, bm, (((1,), (1,)), ((), ())), preferred_element_type=F32)
    li = lax.broadcasted_iota(jnp.int32, (l, l), 0)
    si = lax.broadcasted_iota(jnp.int32, (l, l), 1)
    lower = li >= si
    upper = si >= li
    lane = lax.broadcasted_iota(jnp.int32, (l, 2 * pdim), 1)
    neg = jnp.float32(-jnp.inf)
    ys = []
    for q in range(hpg // 2):
        xp = x[:, q * 2 * pdim:(q + 1) * 2 * pdim]
        rs = []
        for h in (2 * q, 2 * q + 1):
            dfw = jnp.exp(jnp.where(lower, cumf[:, h:h + 1] - pt[hpg + h:hpg + h + 1, :], neg)) * pt[h:h + 1, :]
            dbw = (jnp.exp(jnp.where(upper, pt[3 * hpg + h:3 * hpg + h + 1, :] - cumb[:, h:h + 1], neg))
                   * pt[2 * hpg + h:2 * hpg + h + 1, :])
            mh = (cb * (dfw + dbw)).astype(BF16)
            rs.append(jnp.dot(mh, xp, preferred_element_type=F32))
        ys.append(jnp.where(lane < pdim, rs[0], rs[1]))
    ydiag = jnp.concatenate(ys, axis=1)
    yoff = jnp.dot(cm, h_ref[0].astype(BF16), preferred_element_type=F32) * expand(jnp.exp(cumf))
    yf_ref[...] = ydiag + yoff + dsk_ref[...] * x32
    state_update(0, x32, bm, dtf * jnp.exp(totf - cumf), tef_ref[0, 0])

    xb32 = xb_ref[...].astype(F32)
    pb = pb_ref[0]
    dtb2, cumb2 = pb[:, 2 * hpg:3 * hpg], pb[:, 3 * hpg:4 * hpg]
    totb = totb_ref[0, 0][:, hpg:2 * hpg]
    yb_ref[...] = (jnp.dot(cb_ref[...], h_ref[1].astype(BF16), preferred_element_type=F32)
                   * expand(jnp.exp(totb - cumb2)))
    state_update(1, xb32, bb_ref[...], dtb2 * jnp.exp(cumb2), teb_ref[0, 0])


def _ssd_scan(xbc, dt, cum, tot, d_skip, ctx_len):
    t = xbc.shape[0]
    g, n, pdim, lc = SSD_GROUPS, SSD_STATE, SSD_HEAD_DIM, SSD_CHUNK
    heads = dt.shape[1] // 2
    hpg = heads // g
    gw = hpg * pdim
    nc, ncc = t // lc, ctx_len // lc
    assert (g * gw) % n == 0
    b_off, c_off = g * gw // n, g * gw // n + g

    dt4 = dt.reshape(t, 2, g, hpg)
    cum4 = cum.reshape(t, 2, g, hpg)
    pk = jnp.concatenate([dt4[:, 0], cum4[:, 0], dt4[:, 1], cum4[:, 1]], axis=-1)
    pk = jnp.transpose(pk, (1, 0, 2))
    pkt = jnp.transpose(pk, (0, 2, 1))
    tot4 = tot.reshape(nc, 2, g, hpg)
    totg = jnp.transpose(jnp.concatenate([tot4[:, 0], tot4[:, 1]], axis=-1), (1, 0, 2))
    totg = totg.reshape(g, nc, 1, 2 * hpg)
    tote = jnp.repeat(jnp.exp(tot4), pdim, axis=-1).reshape(nc, 2, 1, g * gw)
    e = jnp.repeat(jnp.eye(hpg, dtype=BF16), pdim, axis=1)
    dsk = jnp.repeat(d_skip.astype(F32), pdim).reshape(1, g * gw)

    def bwd_chunk(j):
        return jnp.where(j < ncc, ncc - 1 - j, nc - 1 - (j - ncc))

    def fwd_chunk(j):
        return j

    in_specs = []
    for direction, chunk in enumerate((fwd_chunk, bwd_chunk)):
        in_specs += [
            pl.BlockSpec((lc, gw), lambda gi, j, c=chunk: (c(j), gi)),
            pl.BlockSpec((lc, n), lambda gi, j, c=chunk: (c(j), b_off + gi)),
            pl.BlockSpec((lc, n), lambda gi, j, c=chunk: (c(j), c_off + gi)),
            pl.BlockSpec((1, lc, 4 * hpg), lambda gi, j, c=chunk: (gi, c(j), 0)),
        ]
        if direction == 0:
            in_specs.append(pl.BlockSpec((1, 4 * hpg, lc), lambda gi, j: (gi, 0, j)))
        in_specs += [
            pl.BlockSpec((1, 1, 1, 2 * hpg), lambda gi, j, c=chunk: (gi, c(j), 0, 0)),
            pl.BlockSpec((1, 1, 1, gw), lambda gi, j, c=chunk, dr=direction: (c(j), dr, 0, gi)),
        ]
    in_specs += [pl.BlockSpec((hpg, gw), lambda gi, j: (0, 0)),
                 pl.BlockSpec((1, gw), lambda gi, j: (0, gi))]
    out_specs = [pl.BlockSpec((lc, gw), lambda gi, j: (j, gi)),
                 pl.BlockSpec((lc, gw), lambda gi, j: (bwd_chunk(j), gi))]
    return pl.pallas_call(
        functools.partial(_ssd_scan_kernel, hpg=hpg),
        grid=(g, nc),
        in_specs=in_specs,
        out_specs=out_specs,
        out_shape=[jax.ShapeDtypeStruct((t, g * gw), F32)] * 2,
        scratch_shapes=[pltpu.VMEM((2, n, gw), F32)],
        compiler_params=_params("parallel", "arbitrary"),
    )(xbc, xbc, xbc, pk, pkt, totg, tote,
      xbc, xbc, xbc, pk, totg, tote, e, dsk)


def _ssd_finish_kernel(yf_ref, yb_ref, z_ref, w_ref, o_ref):
    y = (yf_ref[...] + yb_ref[...]) * _silu(z_ref[...].astype(F32))
    y = y * lax.rsqrt(jnp.mean(y * y, axis=-1, keepdims=True) + RMS_EPS)
    o_ref[...] = (y * w_ref[...]).astype(o_ref.dtype)


def _ssd_finish(yf, yb, z, norm_w):
    t, c = yf.shape
    gw = c // SSD_GROUPS
    tm = _pick(t, (256, 128))
    blk = pl.BlockSpec((tm, gw), lambda i, gi: (i, gi))
    return pl.pallas_call(
        _ssd_finish_kernel,
        grid=(t // tm, SSD_GROUPS),
        in_specs=[blk, blk, blk, pl.BlockSpec((1, gw), lambda i, gi: (0, gi))],
        out_specs=blk,
        out_shape=jax.ShapeDtypeStruct((t, c), BF16),
        compiler_params=_params("parallel", "parallel"),
    )(yf, yb, z, norm_w.reshape(1, c))


def _ssd_mixer(h, xa, gate, w_in, conv_w, conv_b, a_log, dt_bias, d_skip, norm_w, w_out, ctx_len):
    d_inner = w_out.shape[0]
    conv_ch = conv_w.shape[1]
    wz = w_in[:, :d_inner].astype(BF16)
    wx = w_in[:, d_inner:d_inner + conv_ch].astype(BF16)
    wdt = w_in[:, d_inner + conv_ch:].astype(BF16)
    z = _matmul(h, wz, out_dtype=BF16)
    xbc_raw = _matmul(h, wx, out_dtype=F32)
    dt_raw = _matmul(h, wdt, out_dtype=F32)
    xbc = _conv_silu(xbc_raw, conv_w, conv_b, ctx_len, BF16)
    a_neg = -jnp.exp(a_log.astype(F32)).reshape(-1)
    dt, cum, tot = _ssd_dt(dt_raw, dt_bias.astype(F32).reshape(-1), a_neg)
    yf, yb = _ssd_scan(xbc, dt, cum, tot, d_skip, ctx_len)
    yn = _ssd_finish(yf, yb, z, norm_w)
    return _matmul(yn, w_out.astype(BF16), mode="res", res=xa, gate=gate, ctx_len=ctx_len)


def _cpow(ar, ai, n):
    rr, ri = None, None
    br, bi = ar, ai
    while n:
        if n & 1:
            if rr is None:
                rr, ri = br, bi
            else:
                rr, ri = rr * br - ri * bi, rr * bi + ri * br
        n >>= 1
        if n:
            br, bi = br * br - bi * bi, 2.0 * br * bi
    return rr, ri


def _s5_kernel(u_ref, wb_ref, wc_ref, a_ref, dsk_ref, o_ref, yacc_ref, sbuf_ref, *, ctx_len, tt):
    total = u_ref.shape[0]
    half = wb_ref.shape[-1] // 2
    nseg = S5_SEGMENTS
    yacc_ref[...] = jnp.zeros_like(yacc_ref)
    rowid = lax.broadcasted_iota(jnp.int32, (nseg, half), 0)

    for d in range(2):
        wb = wb_ref[d, 0]
        wc = wc_ref[d, 0]
        a = a_ref[d, 0]
        a1r, a1i = a[:, :half], a[:, half:]
        ar = jnp.broadcast_to(a1r, (nseg, half))
        ai = jnp.broadcast_to(a1i, (nseg, half))
        hr = jnp.zeros((1, half), F32)
        hi = jnp.zeros((1, half), F32)
        for base, n in ((0, ctx_len), (ctx_len, total - ctx_len)):
            ls = n // nseg
            ntile = ls // tt

            def rows(k, t, ls=ls, base=base):
                q = k * tt + t
                pos = q if d == 0 else ls - 1 - q
                return pl.ds(base + pos, nseg, stride=ls)

            def project_in(k, rows=rows):
                lhs = jnp.concatenate([u_ref[rows(k, t), :] for t in range(tt)], axis=0)
                return jnp.dot(lhs.astype(BF16), wb, preferred_element_type=F32)

            def step(bu, t, sr, si):
                br = bu[t * nseg:(t + 1) * nseg, :half]
                bi = bu[t * nseg:(t + 1) * nseg, half:]
                return ar * sr - ai * si + br, ar * si + ai * sr + bi

            def pass1(k, carry, project_in=project_in, step=step):
                sr, si = carry
                bu = project_in(k)
                for t in range(tt):
                    sr, si = step(bu, t, sr, si)
                return sr, si

            zero = jnp.zeros((nseg, half), F32)
            sfr, sfi = lax.fori_loop(0, ntile, pass1, (zero, zero))

            alr, ali = _cpow(a1r, a1i, ls)
            hinr, hini = zero, zero
            for sg in (range(nseg) if d == 0 else reversed(range(nseg))):
                hinr = jnp.where(rowid == sg, hr, hinr)
                hini = jnp.where(rowid == sg, hi, hini)
                hr, hi = (alr * hr - ali * hi + sfr[sg:sg + 1], alr * hi + ali * hr + sfi[sg:sg + 1])

            def pass2(k, carry, project_in=project_in, step=step, rows=rows):
                sr, si = carry
                bu = project_in(k)
                for t in range(tt):
                    sr, si = step(bu, t, sr, si)
                    sbuf_ref[t * nseg:(t + 1) * nseg, :half] = sr
                    sbuf_ref[t * nseg:(t + 1) * nseg, half:] = si
                y = jnp.dot(sbuf_ref[...].astype(BF16), wc, preferred_element_type=F32)
                for t in range(tt):
                    idx = rows(k, t)
                    yacc_ref[idx, :] = yacc_ref[idx, :] + y[t * nseg:(t + 1) * nseg]
                return sr, si

            lax.fori_loop(0, ntile, pass2, (hinr, hini))

    y = yacc_ref[...] + dsk_ref[...] * u_ref[...]
    o_ref[...] = jax.nn.gelu(y).astype(o_ref.dtype)


def _s5_weights(lam_re, lam_im, log_step, b_re, b_im, c_re, c_im):
    step = jnp.exp(log_step.astype(F32))[..., None]
    lr, li = lam_re.astype(F32), lam_im.astype(F32)
    mag = jnp.exp(lr * step)
    ar, ai = mag * jnp.cos(li * step), mag * jnp.sin(li * step)
    den = lr * lr + li * li
    kr = ((ar - 1.0) * lr + ai * li) / den
    ki = (ai * lr - (ar - 1.0) * li) / den
    br, bi = b_re.astype(F32), b_im.astype(F32)
    bbr = kr[..., None] * br - ki[..., None] * bi
    bbi = kr[..., None] * bi + ki[..., None] * br
    cr, ci = c_re.astype(F32), c_im.astype(F32)
    ng, pst, gc = bbr.shape[1:]
    gpb = LANES // gc
    nb = ng // gpb
    eye = jnp.eye(gpb, dtype=F32)

    def in_w(bb):
        bb = bb.reshape(2, nb, gpb, pst, gc)
        return jnp.einsum('dbgpc,gh->dbgchp', bb, eye).reshape(2, nb, gpb * gc, gpb * pst)

    def out_w(cc):
        cc = cc.reshape(2, nb, gpb, gc, pst)
        return jnp.einsum('dbgcp,gh->dbhpgc', cc, eye).reshape(2, nb, gpb * pst, gpb * gc)

    wb = jnp.concatenate([in_w(bbr), in_w(bbi)], axis=-1).astype(BF16)
    wc = jnp.concatenate([out_w(cr), out_w(-ci)], axis=-2).astype(BF16)
    avec = jnp.concatenate([ar.reshape(2, nb, 1, gpb * pst), ai.reshape(2, nb, 1, gpb * pst)], axis=-1)
    return wb, wc, avec


def _s5_scan(u, wb, wc, avec, d_skip, ctx_len):
    t, d = u.shape
    nb = d // LANES
    feat = wb.shape[-1]
    tt = S5_TIME_TILE
    assert (ctx_len // S5_SEGMENTS) % tt == 0 and ((t - ctx_len) // S5_SEGMENTS) % tt == 0
    return pl.pallas_call(
        functools.partial(_s5_kernel, ctx_len=ctx_len, tt=tt),
        grid=(nb,),
        in_specs=[
            pl.BlockSpec((t, LANES), lambda b: (0, b)),
            pl.BlockSpec((2, 1, LANES, feat), lambda b: (0, b, 0, 0)),
            pl.BlockSpec((2, 1, feat, LANES), lambda b: (0, b, 0, 0)),
            pl.BlockSpec((2, 1, 1, feat), lambda b: (0, b, 0, 0)),
            pl.BlockSpec((1, LANES), lambda b: (0, b)),
        ],
        out_specs=pl.BlockSpec((t, LANES), lambda b: (0, b)),
        out_shape=jax.ShapeDtypeStruct((t, d), BF16),
        scratch_shapes=[pltpu.VMEM((t, LANES), F32), pltpu.VMEM((S5_SEGMENTS * tt, feat), F32)],
        compiler_params=_params("parallel"),
    )(u, wb, wc, avec, d_skip.astype(F32).reshape(1, d))


def _s5_mixer(u, xa, gate, lam_re, lam_im, log_step, b_re, b_im, c_re, c_im, d_skip, glu_w, glu_b, ctx_len):
    wb, wc, avec = _s5_weights(lam_re, lam_im, log_step, b_re, b_im, c_re, c_im)
    y = _s5_scan(u, wb, wc, avec, d_skip, ctx_len)
    return _matmul(y, glu_w.astype(BF16), mode="glu", bias=glu_b.astype(F32), res=xa, gate=gate,
                   ctx_len=ctx_len)


def _qk_prep_kernel(x_ref, g_ref, cos_ref, sin_ref, o_ref, *, dh, scale):
    cos = cos_ref[...]
    sin = sin_ref[...]
    g = g_ref[...]
    lane = lax.broadcasted_iota(jnp.int32, cos.shape, 1)
    quarter = dh // 4
    first = (lane // quarter) % 2 == 0
    for s in range(x_ref.shape[1] // dh):
        x = x_ref[:, s * dh:(s + 1) * dh]
        y = x * lax.rsqrt(jnp.mean(x * x, axis=-1, keepdims=True) + RMS_EPS) * g
        partner = jnp.where(first, pltpu.roll(y, dh - quarter, axis=1), pltpu.roll(y, quarter, axis=1))
        o_ref[:, s * dh:(s + 1) * dh] = ((y * cos + partner * sin) * scale).astype(o_ref.dtype)


def _qk_prep(x, gain, cos, sin, first_block, nblk, scale):
    t = x.shape[0]
    dh = gain.shape[0]
    width = 4 * dh
    tm = _pick(t, (768, 512, 256))
    return pl.pallas_call(
        functools.partial(_qk_prep_kernel, dh=dh, scale=scale),
        grid=(t // tm, nblk),
        in_specs=[
            pl.BlockSpec((tm, width), lambda i, j: (i, j + first_block)),
            pl.BlockSpec((1, dh), lambda i, j: (0, 0)),
            pl.BlockSpec((tm, dh), lambda i, j: (i, 0)),
            pl.BlockSpec((tm, dh), lambda i, j: (i, 0)),
        ],
        out_specs=pl.BlockSpec((tm, width), lambda i, j: (i, j)),
        out_shape=jax.ShapeDtypeStruct((t, nblk * width), BF16),
        compiler_params=_params("parallel", "parallel"),
    )(x, gain.astype(F32).reshape(1, dh), cos, sin)


def _flash_kernel(q_ref, k_ref, v_ref, lam_ref, g_ref, o_ref, m_ref, l_ref, acc_ref, *, ctx_len, tk, dh,
                  post_scale):
    i = pl.program_id(1)
    total = k_ref.shape[0]
    m_ref[...] = jnp.full_like(m_ref, -jnp.inf)
    l_ref[...] = jnp.zeros_like(l_ref)
    acc_ref[...] = jnp.zeros_like(acc_ref)

    def visit(start, size):
        k = k_ref[pl.ds(start, size), :]
        v = v_ref[pl.ds(start, size), :]
        for j in range(2):
            q = q_ref[:, j * dh:(j + 1) * dh]
            s = lax.dot_general(q, k[:, j * dh:(j + 1) * dh], (((1,), (1,)), ((), ())),
                                preferred_element_type=F32)
            m_old = m_ref[j]
            m_new = jnp.maximum(m_old, jnp.max(s, axis=-1, keepdims=True))
            alpha = jnp.exp(m_old - m_new)
            p = jnp.exp(s - m_new)
            l_ref[j] = alpha * l_ref[j] + jnp.sum(p, axis=-1, keepdims=True)
            acc_ref[j] = alpha * acc_ref[j] + jnp.dot(p.astype(BF16), v, preferred_element_type=F32)
            m_ref[j] = m_new

    visit(0, ctx_len)
    nlat = jnp.where(i == 0, 0, (total - ctx_len) // tk)

    def body(c, carry):
        visit(pl.multiple_of(ctx_len + c * tk, tk), tk)
        return carry

    lax.fori_loop(0, nlat, body, 0)
    o = acc_ref[0] / l_ref[0] - lam_ref[...] * (acc_ref[1] / l_ref[1])
    o = o * lax.rsqrt(jnp.mean(o * o, axis=-1, keepdims=True) + RMS_EPS) * g_ref[...] * post_scale
    o_ref[...] = o.astype(o_ref.dtype)


def _flash(q, k, v, lam, sub_gain, ctx_len, post_scale):
    t, d = q.shape
    dh = sub_gain.shape[0] // 2
    hw = 2 * dh
    tq = ctx_len
    tk = _pick(t - ctx_len, (512, 256, 128))
    lamv = jnp.full((1, hw), lam, F32)
    return pl.pallas_call(
        functools.partial(_flash_kernel, ctx_len=ctx_len, tk=tk, dh=dh, post_scale=post_scale),
        grid=(d // hw, t // tq),
        in_specs=[
            pl.BlockSpec((tq, hw), lambda h, i: (i, h)),
            pl.BlockSpec((t, hw), lambda h, i: (0, h)),
            pl.BlockSpec((t, hw), lambda h, i: (0, h)),
            pl.BlockSpec((1, hw), lambda h, i: (0, 0)),
            pl.BlockSpec((1, hw), lambda h, i: (0, 0)),
        ],
        out_specs=pl.BlockSpec((tq, hw), lambda h, i: (i, h)),
        out_shape=jax.ShapeDtypeStruct((t, d), BF16),
        scratch_shapes=[pltpu.VMEM((2, tq, 1), F32), pltpu.VMEM((2, tq, 1), F32), pltpu.VMEM((2, tq, hw), F32)],
        compiler_params=_params("parallel", "arbitrary"),
    )(q, k, v, lamv, sub_gain.astype(F32).reshape(1, hw))


def _rope_tables(seq, ctx_len, dh):
    rows = seq // GRID_W
    row = jnp.repeat(jnp.arange(rows, dtype=F32), GRID_W)
    col = jnp.tile(jnp.arange(GRID_W, dtype=F32), rows)
    n_freq = dh // 4
    inv_freq = ROPE_BASE ** (-jnp.arange(n_freq, dtype=F32) / n_freq)
    ang_r = row[:, None] * inv_freq
    ang_c = col[:, None] * inv_freq
    cos = jnp.concatenate([jnp.cos(ang_r), jnp.cos(ang_r), jnp.cos(ang_c), jnp.cos(ang_c)], axis=-1)
    sin = jnp.concatenate([-jnp.sin(ang_r), jnp.sin(ang_r), -jnp.sin(ang_c), jnp.sin(ang_c)], axis=-1)
    cos = jnp.concatenate([jnp.ones((ctx_len, dh), F32), cos], axis=0)
    sin = jnp.concatenate([jnp.zeros((ctx_len, dh), F32), sin], axis=0)
    return cos, sin


def _diff_attn_mixer(h, xa, gate, w_q, w_k, w_v, w_o, q_gain, k_gain, lam_q1, lam_k1, lam_q2, lam_k2,
                     sub_gain, lambda_init, ctx_len):
    t, d = h.shape
    dh = q_gain.shape[0]
    wqk = jnp.concatenate([w_q, w_k], axis=1).astype(BF16)
    qk = _matmul(h, wqk, out_dtype=F32)
    v = _matmul(h, w_v.astype(BF16), out_dtype=BF16)
    cos, sin = _rope_tables(t - ctx_len, ctx_len, dh)
    nblk = d // (4 * dh)
    q = _qk_prep(qk, q_gain, cos, sin, 0, nblk, dh ** -0.5)
    k = _qk_prep(qk, k_gain, cos, sin, nblk, nblk, 1.0)
    lam = (jnp.exp(jnp.sum(lam_q1.astype(F32) * lam_k1.astype(F32)))
           - jnp.exp(jnp.sum(lam_q2.astype(F32) * lam_k2.astype(F32))) + lambda_init)
    o = _flash(q, k, v, lam, sub_gain, ctx_len, 1.0 - lambda_init)
    return _matmul(o, w_o.astype(BF16), mode="res", res=xa, gate=gate, ctx_len=ctx_len)


def _conv_ffn(hf, xa, gate, w_up, conv_w, conv_b, w_down, ctx_len):
    a = _matmul(hf, w_up.astype(BF16), out_dtype=F32)
    tgt = _conv_glu(a, conv_w.astype(F32), conv_b.astype(F32), ctx_len, BF16)
    return _matmul(tgt, w_down.astype(BF16), mode="res", res=xa, gate=gate, ctx_len=ctx_len)


def kernel(x, c, ctx, c_ctx, mod_w, mod_b, norm_mix, norm_ffn, ffn_up, ffn_conv_w, ffn_conv_b, ffn_down, ssd_w_in, ssd_conv_w, ssd_conv_b, ssd_a_log, ssd_dt_bias, ssd_d, ssd_norm, ssd_w_out, s5_lam_re, s5_lam_im, s5_log_step, s5_b_re, s5_b_im, s5_c_re, s5_c_im, s5_d, s5_glu_w, s5_glu_b, da_w_q, da_w_k, da_w_v, da_w_o, da_q_norm, da_k_norm, da_lam_q1, da_lam_k1, da_lam_q2, da_lam_k2, da_sub_norm):
    bsz, seq, d = x.shape
    assert bsz == 1
    ctx_len = ctx.shape[1]
    depth = mod_w.shape[0]
    cond = jnp.concatenate([jax.nn.silu(c.astype(F32)), jax.nn.silu(c_ctx.astype(F32))[None, :]], axis=0)
    mod = _modulation(cond, mod_w, mod_b).reshape(depth, 2, 6, d)
    xa = jnp.concatenate([ctx[0], x[0]], axis=0).astype(F32)

    for i in range(depth):
        kind, j = i % N_MIXERS, i // N_MIXERS
        m = mod[i]
        g1, g2 = m[:, 2], m[:, 5]
        h = _norm_mod(xa, norm_mix[i], m, 0, 1, ctx_len, F32 if kind == 1 else BF16)
        if kind == 0:
            xa = _ssd_mixer(h, xa, g1, ssd_w_in[j], ssd_conv_w[j], ssd_conv_b[j], ssd_a_log[j], ssd_dt_bias[j],
                            ssd_d[j], ssd_norm[j], ssd_w_out[j], ctx_len)
        elif kind == 1:
            xa = _s5_mixer(h, xa, g1, s5_lam_re[j], s5_lam_im[j], s5_log_step[j], s5_b_re[j], s5_b_im[j],
                           s5_c_re[j], s5_c_im[j], s5_d[j], s5_glu_w[j], s5_glu_b[j], ctx_len)
        else:
            lambda_init = 0.8 - 0.6 * math.exp(-0.3 * i)
            xa = _diff_attn_mixer(h, xa, g1, da_w_q[j], da_w_k[j], da_w_v[j], da_w_o[j], da_q_norm[j],
                                  da_k_norm[j], da_lam_q1[j], da_lam_k1[j], da_lam_q2[j], da_lam_k2[j],
                                  da_sub_norm[j], lambda_init, ctx_len)
        hf = _norm_mod(xa, norm_ffn[i], m, 3, 4, ctx_len, BF16)
        xa = _conv_ffn(hf, xa, g2, ffn_up[i], ffn_conv_w[i], ffn_conv_b[i], ffn_down[i], ctx_len)
    return xa[ctx_len:][None]
```

```python
import functools
import math

import jax
import jax.numpy as jnp
from jax import lax
from jax.experimental import pallas as pl
from jax.experimental.pallas import tpu as pltpu

F32 = jnp.float32
BF16 = jnp.bfloat16

RMS_EPS = 1e-6
ROPE_BASE = 10000.0
GRID_W = 64
N_MIXERS = 3

LANES = 128
SUBLANES = 8
VMEM_LIMIT_BYTES = 52 * 1024 * 1024

SSD_GROUPS = 8
SSD_HEAD_DIM = 64
SSD_STATE = 128
SSD_CHUNK = 128
S5_SEGMENTS = SUBLANES
S5_TIME_TILE = 32


def _params(*semantics):
    return pltpu.CompilerParams(dimension_semantics=semantics, vmem_limit_bytes=VMEM_LIMIT_BYTES)


def _pick(n, candidates):
    for c in candidates:
        if n % c == 0:
            return c
    raise ValueError(f"no tile in {candidates} divides {n}")


def _sigmoid(x):
    return 1.0 / (1.0 + jnp.exp(-x))


def _silu(x):
    return x * _sigmoid(x)


def _mod_kernel(c_ref, w_ref, b_ref, o_ref, acc_ref, *, nk):
    k = pl.program_id(2)

    @pl.when(k == 0)
    def _():
        acc_ref[...] = jnp.zeros_like(acc_ref)

    w = w_ref[0]
    tk, tn = w.shape
    for r in range(2):
        cb = jnp.tile(c_ref[r], (1, tn // LANES))
        acc_ref[r] += (w * cb).reshape(tk // SUBLANES, SUBLANES, tn).sum(axis=0)

    @pl.when(k == nk - 1)
    def _():
        o_ref[0] = acc_ref[...].sum(axis=1) + b_ref[0]


def _modulation(cond, mod_w, mod_b):
    depth, d, n = mod_w.shape
    tk = _pick(d, (512, 256, 128))
    tn = _pick(n, (2048, 1024, 512))
    nk = d // tk
    condb = jnp.broadcast_to(cond[:, :, None], (2, d, LANES))
    return pl.pallas_call(
        functools.partial(_mod_kernel, nk=nk),
        grid=(depth, n // tn, nk),
        in_specs=[
            pl.BlockSpec((2, tk, LANES), lambda l, j, k: (0, k, 0)),
            pl.BlockSpec((1, tk, tn), lambda l, j, k: (l, k, j)),
            pl.BlockSpec((1, 1, tn), lambda l, j, k: (l, 0, j)),
        ],
        out_specs=pl.BlockSpec((1, 2, tn), lambda l, j, k: (l, 0, j)),
        out_shape=jax.ShapeDtypeStruct((depth, 2, n), F32),
        scratch_shapes=[pltpu.VMEM((2, SUBLANES, tn), F32)],
        compiler_params=_params("parallel", "parallel", "arbitrary"),
    )(condb, mod_w, mod_b.reshape(depth, 1, n))


def _norm_mod_kernel(x_ref, g_ref, m_ref, o_ref, *, shift_row, scale_row):
    x = x_ref[...]
    y = x * lax.rsqrt(jnp.mean(x * x, axis=-1, keepdims=True) + RMS_EPS) * g_ref[...]
    m = m_ref[0]
    o_ref[...] = (y * (1.0 + m[scale_row:scale_row + 1]) + m[shift_row:shift_row + 1]).astype(o_ref.dtype)


def _norm_mod(xa, gain, mod, shift_row, scale_row, ctx_len, out_dtype):
    t, d = xa.shape
    tm = _pick(ctx_len, (256, 128))
    ncb = ctx_len // tm
    return pl.pallas_call(
        functools.partial(_norm_mod_kernel, shift_row=shift_row, scale_row=scale_row),
        grid=(t // tm,),
        in_specs=[
            pl.BlockSpec((tm, d), lambda i: (i, 0)),
            pl.BlockSpec((1, d), lambda i: (0, 0)),
            pl.BlockSpec((1, 6, d), lambda i: (jnp.where(i < ncb, 1, 0), 0, 0)),
        ],
        out_specs=pl.BlockSpec((tm, d), lambda i: (i, 0)),
        out_shape=jax.ShapeDtypeStruct((t, d), out_dtype),
        compiler_params=_params("parallel"),
    )(xa, gain.reshape(1, d), mod)


def _gate_rows(gate_ref, i, tm, tn, ctx_len):
    rows = i * tm + lax.broadcasted_iota(jnp.int32, (tm, tn), 0)
    return jnp.where(rows < ctx_len, gate_ref[1:2, :], gate_ref[0:1, :])


def _mm_kernel(*refs, nk, mode, ctx_len, cast_w):
    i = pl.program_id(1)
    k = pl.program_id(2)
    nw = 2 if mode == "glu" else 1
    a_ref, w_refs, rest = refs[0], refs[1:1 + nw], refs[1 + nw:]
    if mode == "plain":
        o_ref, scratch = rest[0], rest[1:]
    elif mode == "res":
        (res_ref, gate_ref, o_ref), scratch = rest[:3], rest[3:]
    else:
        (ba_ref, bb_ref, res_ref, gate_ref, o_ref), scratch = rest[:5], rest[5:]

    if cast_w:
        wbf, scratch = scratch[:nw], scratch[nw:]

        @pl.when(i == 0)
        def _():
            for dst, src in zip(wbf, w_refs):
                dst[...] = src[...].astype(BF16)

        w_refs = wbf
    accs = scratch

    a = a_ref[...]
    prods = [jnp.dot(a, w[...], preferred_element_type=F32) for w in w_refs]

    def finish(vals):
        tm, tn = vals[0].shape
        if mode == "plain":
            o_ref[...] = vals[0].astype(o_ref.dtype)
        elif mode == "res":
            o_ref[...] = res_ref[...] + _gate_rows(gate_ref, i, tm, tn, ctx_len) * vals[0]
        else:
            o = (vals[0] + ba_ref[...]) * _sigmoid(vals[1] + bb_ref[...])
            o_ref[...] = res_ref[...] + _gate_rows(gate_ref, i, tm, tn, ctx_len) * o

    if nk == 1:
        finish(prods)
    else:
        @pl.when(k == 0)
        def _():
            for acc, p in zip(accs, prods):
                acc[...] = p

        @pl.when(k > 0)
        def _():
            for acc, p in zip(accs, prods):
                acc[...] += p

        @pl.when(k == nk - 1)
        def _():
            finish([acc[...] for acc in accs])


def _matmul(a, w, *, n=None, col_off=0, out_dtype=F32, mode="plain", res=None, gate=None, bias=None, ctx_len=0):
    m, kdim = a.shape
    n_w = w.shape[1]
    nw = 2 if mode == "glu" else 1
    if n is None:
        n = n_w // nw
    tm = _pick(m, (768, 512, 256))
    tk = _pick(kdim, (4096, 2048, 1024, 512))
    nk = kdim // tk
    cast_w = w.dtype == F32 and nk == 1
    if w.dtype == F32 and not cast_w:
        w = w.astype(BF16)
    tn = n if n < LANES else _pick(n, (256, 128) if (cast_w and nw == 2) else (512, 256, 128))
    assert col_off % tn == 0
    nb, cb = n // tn, col_off // tn
    a_spec = pl.BlockSpec((tm, tk), lambda j, i, k: (i, k))
    o_spec = pl.BlockSpec((tm, tn), lambda j, i, k: (i, j))
    w_spec = pl.BlockSpec((tk, tn), lambda j, i, k: (k, j + cb))
    row_spec = pl.BlockSpec((1, tn), lambda j, i, k: (0, j))
    gate_spec = pl.BlockSpec((2, tn), lambda j, i, k: (0, j))
    if mode == "plain":
        in_specs, args = [a_spec, w_spec], (a, w)
    elif mode == "res":
        in_specs, args = [a_spec, w_spec, o_spec, gate_spec], (a, w, res, gate)
    else:
        wb_spec = pl.BlockSpec((tk, tn), lambda j, i, k: (k, j + nb))
        bb_spec = pl.BlockSpec((1, tn), lambda j, i, k: (0, j + nb))
        b2 = bias.reshape(1, n_w)
        in_specs = [a_spec, w_spec, wb_spec, row_spec, bb_spec, o_spec, gate_spec]
        args = (a, w, w, b2, b2, res, gate)
    scratch = [pltpu.VMEM((tk, tn), BF16)] * nw if cast_w else []
    scratch += [pltpu.VMEM((tm, tn), F32)] * nw if nk > 1 else []
    semantics = ("arbitrary",) * 3 if cast_w else ("parallel", "parallel", "arbitrary")
    return pl.pallas_call(
        functools.partial(_mm_kernel, nk=nk, mode=mode, ctx_len=ctx_len, cast_w=cast_w),
        grid=(nb, m // tm, nk),
        in_specs=in_specs,
        out_specs=o_spec,
        out_shape=jax.ShapeDtypeStruct((m, n), out_dtype),
        scratch_shapes=scratch,
        compiler_params=_params(*semantics),
    )(*args)


def _conv_window(cur_ref, prev_ref, next_ref, w_ref, b_ref, ext_ref, i, tm, ctx_len, total):
    taps = w_ref.shape[0]
    r = taps // 2
    start = i * tm
    at_start = jnp.logical_or(start == 0, start == ctx_len)
    at_end = jnp.logical_or(start + tm == ctx_len, start + tm == total)
    ext_ref[0:SUBLANES, :] = jnp.where(at_start, 0.0, prev_ref[...].astype(F32))
    ext_ref[SUBLANES:SUBLANES + tm, :] = cur_ref[...].astype(F32)
    ext_ref[SUBLANES + tm:, :] = jnp.where(at_end, 0.0, next_ref[...].astype(F32))
    out = b_ref[...] + ext_ref[SUBLANES - r:SUBLANES - r + tm, :] * w_ref[0:1, :]
    for j in range(1, taps):
        off = SUBLANES - r + j
        out = out + ext_ref[off:off + tm, :] * w_ref[j:j + 1, :]
    return out


def _conv_silu_kernel(cur_ref, prev_ref, next_ref, w_ref, b_ref, o_ref, ext_ref, *, tm, ctx_len, total):
    i = pl.program_id(0)
    o_ref[...] = _silu(_conv_window(cur_ref, prev_ref, next_ref, w_ref, b_ref, ext_ref, i, tm, ctx_len, total)
                       ).astype(o_ref.dtype)


def _conv_glu_kernel(gc_ref, gp_ref, gn_ref, vc_ref, vp_ref, vn_ref, wg_ref, wv_ref, bg_ref, bv_ref, o_ref,
                     extg_ref, extv_ref, *, tm, ctx_len, total):
    i = pl.program_id(0)
    g = _conv_window(gc_ref, gp_ref, gn_ref, wg_ref, bg_ref, extg_ref, i, tm, ctx_len, total)
    v = _conv_window(vc_ref, vp_ref, vn_ref, wv_ref, bv_ref, extv_ref, i, tm, ctx_len, total)
    o_ref[...] = (_silu(g) * v).astype(o_ref.dtype)


def _conv_specs(tm, tc, taps, nrow8, col_off):
    hb = tm // SUBLANES
    cur = pl.BlockSpec((tm, tc), lambda i, j: (i, j + col_off))
    prev = pl.BlockSpec((SUBLANES, tc), lambda i, j: (jnp.maximum(i * hb - 1, 0), j + col_off))
    nxt = pl.BlockSpec((SUBLANES, tc), lambda i, j: (jnp.minimum((i + 1) * hb, nrow8 - 1), j + col_off))
    w = pl.BlockSpec((taps, tc), lambda i, j: (0, j + col_off))
    b = pl.BlockSpec((1, tc), lambda i, j: (0, j + col_off))
    return cur, prev, nxt, w, b


def _conv_silu(a, w, b, ctx_len, out_dtype):
    t, c = a.shape
    taps = w.shape[0]
    tm = _pick(ctx_len, (256, 128))
    tc = _pick(c, (1024, 512, 256, 128))
    cur, prev, nxt, ws, bs = _conv_specs(tm, tc, taps, t // SUBLANES, 0)
    return pl.pallas_call(
        functools.partial(_conv_silu_kernel, tm=tm, ctx_len=ctx_len, total=t),
        grid=(t // tm, c // tc),
        in_specs=[cur, prev, nxt, ws, bs],
        out_specs=pl.BlockSpec((tm, tc), lambda i, j: (i, j)),
        out_shape=jax.ShapeDtypeStruct((t, c), out_dtype),
        scratch_shapes=[pltpu.VMEM((tm + 2 * SUBLANES, tc), F32)],
        compiler_params=_params("parallel", "parallel"),
    )(a, a, a, w, b.reshape(1, c))


def _conv_glu(a, w, b, ctx_len, out_dtype):
    t, c2 = a.shape
    f = c2 // 2
    taps = w.shape[0]
    tm = _pick(ctx_len, (256, 128))
    tc = _pick(f, (1024, 512, 256, 128))
    g = _conv_specs(tm, tc, taps, t // SUBLANES, 0)
    v = _conv_specs(tm, tc, taps, t // SUBLANES, f // tc)
    b2 = b.reshape(1, c2)
    return pl.pallas_call(
        functools.partial(_conv_glu_kernel, tm=tm, ctx_len=ctx_len, total=t),
        grid=(t // tm, f // tc),
        in_specs=[g[0], g[1], g[2], v[0], v[1], v[2], g[3], v[3], g[4], v[4]],
        out_specs=pl.BlockSpec((tm, tc), lambda i, j: (i, j)),
        out_shape=jax.ShapeDtypeStruct((t, f), out_dtype),
        scratch_shapes=[pltpu.VMEM((tm + 2 * SUBLANES, tc), F32)] * 2,
        compiler_params=_params("parallel", "parallel"),
    )(a, a, a, a, a, a, w, w, b2, b2)


def _split3(v):
    h1 = v.astype(BF16)
    r1 = v - h1.astype(F32)
    h2 = r1.astype(BF16)
    h3 = (r1 - h2.astype(F32)).astype(BF16)
    return h1, h2, h3


def _ssd_dt_kernel(raw_ref, bias_ref, a_ref, dt_ref, cum_ref, tot_ref, *, nfwd):
    x = raw_ref[...] + bias_ref[...]
    dt = jnp.maximum(x, 0.0) + jnp.log(1.0 + jnp.exp(-jnp.abs(x)))
    dt_ref[...] = dt
    da = dt * a_ref[...]
    l = da.shape[0]
    row = lax.broadcasted_iota(jnp.int32, (l, l), 0)
    col = lax.broadcasted_iota(jnp.int32, (l, l), 1)
    incl_m = jnp.where(row >= col, 1.0, 0.0).astype(BF16)
    excl_m = jnp.where(row > col, 1.0, 0.0).astype(BF16)
    parts = _split3(da)
    incl = sum(jnp.dot(incl_m, p, preferred_element_type=F32) for p in parts)
    excl = sum(jnp.dot(excl_m, p, preferred_element_type=F32) for p in parts)
    lane = lax.broadcasted_iota(jnp.int32, da.shape, 1)
    cum_ref[...] = jnp.where(lane < nfwd, incl, excl)
    tot_ref[0] = incl[l - 1:l, :]


def _ssd_dt(dt_raw, dt_bias, a_neg):
    t, c = dt_raw.shape
    nc = t // SSD_CHUNK
    row = pl.BlockSpec((1, c), lambda i: (0, 0))
    blk = pl.BlockSpec((SSD_CHUNK, c), lambda i: (i, 0))
    return pl.pallas_call(
        functools.partial(_ssd_dt_kernel, nfwd=c // 2),
        grid=(nc,),
        in_specs=[blk, row, row],
        out_specs=[blk, blk, pl.BlockSpec((1, 1, c), lambda i: (i, 0, 0))],
        out_shape=[jax.ShapeDtypeStruct((t, c), F32), jax.ShapeDtypeStruct((t, c), F32),
                   jax.ShapeDtypeStruct((nc, 1, c), F32)],
        compiler_params=_params("parallel"),
    )(dt_raw, dt_bias.reshape(1, c), a_neg.reshape(1, c))


def _ssd_scan_kernel(xf_ref, bf_ref, cf_ref, pf_ref, ptf_ref, totf_ref, tef_ref,
                     xb_ref, bb_ref, cb_ref, pb_ref, totb_ref, teb_ref,
                     e_ref, dsk_ref, yf_ref, yb_ref, h_ref, *, hpg):
    pdim = SSD_HEAD_DIM
    j = pl.program_id(1)

    @pl.when(j == 0)
    def _():
        h_ref[...] = jnp.zeros_like(h_ref)

    e = e_ref[...]

    def expand(v):
        return jnp.dot(v.astype(BF16), e, preferred_element_type=F32)

    def state_update(d, x32, bm, scale, total_decay):
        xw = (x32 * expand(scale)).astype(BF16)
        inc = lax.dot_general(bm, xw, (((0,), (0,)), ((), ())), preferred_element_type=F32)
        h_ref[d] = h_ref[d] * total_decay + inc

    x = xf_ref[...]
    x32 = x.astype(F32)
    bm = bf_ref[...]
    cm = cf_ref[...]
    p = pf_ref[0]
    pt = ptf_ref[0]
    dtf, cumf = p[:, 0:hpg], p[:, hpg:2 * hpg]
    cumb = p[:, 3 * hpg:4 * hpg]
    totf = totf_ref[0, 0][:, 0:hpg]
    l = x.shape[0]
    cb = lax.dot_general(cm, bm, (((1,), (1,)), ((), ())), preferred_element_type=F32)
    li = lax.broadcasted_iota(jnp.int32, (l, l), 0)
    si = lax.broadcasted_iota(jnp.int32, (l, l), 1)
    lower = li >= si
    upper = si >= li
    lane = lax.broadcasted_iota(jnp.int32, (l, 2 * pdim), 1)
    neg = jnp.float32(-jnp.inf)
    cm32 = cm.astype(F32)
    hf = h_ref[0].astype(BF16)
    ys = []
    for q in range(hpg // 2):
        rhs = jnp.concatenate([x[:, q * 2 * pdim:(q + 1) * 2 * pdim], hf[:, q * 2 * pdim:(q + 1) * 2 * pdim]],
                              axis=0)
        rs = []
        for h in (2 * q, 2 * q + 1):
            cf_col = jnp.broadcast_to(cumf[:, h:h + 1], (l, l))
            dfw = jnp.exp(jnp.where(lower, cf_col - pt[hpg + h:hpg + h + 1, :], neg)) * pt[h:h + 1, :]
            dbw = (jnp.exp(jnp.where(upper, pt[3 * hpg + h:3 * hpg + h + 1, :] - cumb[:, h:h + 1], neg))
                   * pt[2 * hpg + h:2 * hpg + h + 1, :])
            lhs = jnp.concatenate([(cb * (dfw + dbw)).astype(BF16), (cm32 * jnp.exp(cf_col)).astype(BF16)],
                                  axis=1)
            rs.append(jnp.dot(lhs, rhs, preferred_element_type=F32))
        ys.append(jnp.where(lane < pdim, rs[0], rs[1]))
    yf_ref[...] = jnp.concatenate(ys, axis=1) + dsk_ref[...] * x32
    state_update(0, x32, bm, dtf * jnp.exp(totf - cumf), tef_ref[0, 0])

    xb32 = xb_ref[...].astype(F32)
    pb = pb_ref[0]
    dtb2, cumb2 = pb[:, 2 * hpg:3 * hpg], pb[:, 3 * hpg:4 * hpg]
    totb = totb_ref[0, 0][:, hpg:2 * hpg]
    yb_ref[...] = (jnp.dot(cb_ref[...], h_ref[1].astype(BF16), preferred_element_type=F32)
                   * expand(jnp.exp(totb - cumb2)))
    state_update(1, xb32, bb_ref[...], dtb2 * jnp.exp(cumb2), teb_ref[0, 0])


def _ssd_scan(xbc, dt, cum, tot, d_skip, ctx_len):
    t = xbc.shape[0]
    g, n, pdim, lc = SSD_GROUPS, SSD_STATE, SSD_HEAD_DIM, SSD_CHUNK
    heads = dt.shape[1] // 2
    hpg = heads // g
    gw = hpg * pdim
    nc, ncc = t // lc, ctx_len // lc
    assert (g * gw) % n == 0
    b_off, c_off = g * gw // n, g * gw // n + g

    dt4 = dt.reshape(t, 2, g, hpg)
    cum4 = cum.reshape(t, 2, g, hpg)
    pk = jnp.concatenate([dt4[:, 0], cum4[:, 0], dt4[:, 1], cum4[:, 1]], axis=-1)
    pk = jnp.transpose(pk, (1, 0, 2))
    pkt = jnp.transpose(pk, (0, 2, 1))
    tot4 = tot.reshape(nc, 2, g, hpg)
    totg = jnp.transpose(jnp.concatenate([tot4[:, 0], tot4[:, 1]], axis=-1), (1, 0, 2))
    totg = totg.reshape(g, nc, 1, 2 * hpg)
    tote = jnp.repeat(jnp.exp(tot4), pdim, axis=-1).reshape(nc, 2, 1, g * gw)
    e = jnp.repeat(jnp.eye(hpg, dtype=BF16), pdim, axis=1)
    dsk = jnp.repeat(d_skip.astype(F32), pdim).reshape(1, g * gw)

    def bwd_chunk(j):
        return jnp.where(j < ncc, ncc - 1 - j, nc - 1 - (j - ncc))

    def fwd_chunk(j):
        return j

    in_specs = []
    for direction, chunk in enumerate((fwd_chunk, bwd_chunk)):
        in_specs += [
            pl.BlockSpec((lc, gw), lambda gi, j, c=chunk: (c(j), gi)),
            pl.BlockSpec((lc, n), lambda gi, j, c=chunk: (c(j), b_off + gi)),
            pl.BlockSpec((lc, n), lambda gi, j, c=chunk: (c(j), c_off + gi)),
            pl.BlockSpec((1, lc, 4 * hpg), lambda gi, j, c=chunk: (gi, c(j), 0)),
        ]
        if direction == 0:
            in_specs.append(pl.BlockSpec((1, 4 * hpg, lc), lambda gi, j: (gi, 0, j)))
        in_specs += [
            pl.BlockSpec((1, 1, 1, 2 * hpg), lambda gi, j, c=chunk: (gi, c(j), 0, 0)),
            pl.BlockSpec((1, 1, 1, gw), lambda gi, j, c=chunk, dr=direction: (c(j), dr, 0, gi)),
        ]
    in_specs += [pl.BlockSpec((hpg, gw), lambda gi, j: (0, 0)),
                 pl.BlockSpec((1, gw), lambda gi, j: (0, gi))]
    out_specs = [pl.BlockSpec((lc, gw), lambda gi, j: (j, gi)),
                 pl.BlockSpec((lc, gw), lambda gi, j: (bwd_chunk(j), gi))]
    return pl.pallas_call(
        functools.partial(_ssd_scan_kernel, hpg=hpg),
        grid=(g, nc),
        in_specs=in_specs,
        out_specs=out_specs,
        out_shape=[jax.ShapeDtypeStruct((t, g * gw), F32)] * 2,
        scratch_shapes=[pltpu.VMEM((2, n, gw), F32)],
        compiler_params=_params("parallel", "arbitrary"),
    )(xbc, xbc, xbc, pk, pkt, totg, tote,
      xbc, xbc, xbc, pk, totg, tote, e, dsk)


def _ssd_finish_kernel(yf_ref, yb_ref, z_ref, w_ref, o_ref):
    y = (yf_ref[...] + yb_ref[...]) * _silu(z_ref[...].astype(F32))
    y = y * lax.rsqrt(jnp.mean(y * y, axis=-1, keepdims=True) + RMS_EPS)
    o_ref[...] = (y * w_ref[...]).astype(o_ref.dtype)


def _ssd_finish(yf, yb, z, norm_w):
    t, c = yf.shape
    gw = c // SSD_GROUPS
    tm = _pick(t, (256, 128))
    blk = pl.BlockSpec((tm, gw), lambda i, gi: (i, gi))
    return pl.pallas_call(
        _ssd_finish_kernel,
        grid=(t // tm, SSD_GROUPS),
        in_specs=[blk, blk, blk, pl.BlockSpec((1, gw), lambda i, gi: (0, gi))],
        out_specs=blk,
        out_shape=jax.ShapeDtypeStruct((t, c), BF16),
        compiler_params=_params("parallel", "parallel"),
    )(yf, yb, z, norm_w.reshape(1, c))


def _ssd_mixer(h, xa, gate, w_in, conv_w, conv_b, a_log, dt_bias, d_skip, norm_w, w_out, ctx_len):
    d_inner = w_out.shape[0]
    conv_ch = conv_w.shape[1]
    n_dt = w_in.shape[1] - d_inner - conv_ch
    z = _matmul(h, w_in, n=d_inner, col_off=0, out_dtype=BF16)
    xbc_raw = _matmul(h, w_in, n=conv_ch, col_off=d_inner, out_dtype=F32)
    dt_raw = _matmul(h, w_in, n=n_dt, col_off=d_inner + conv_ch, out_dtype=F32)
    xbc = _conv_silu(xbc_raw, conv_w, conv_b, ctx_len, BF16)
    a_neg = -jnp.exp(a_log.astype(F32)).reshape(-1)
    dt, cum, tot = _ssd_dt(dt_raw, dt_bias.astype(F32).reshape(-1), a_neg)
    yf, yb = _ssd_scan(xbc, dt, cum, tot, d_skip, ctx_len)
    yn = _ssd_finish(yf, yb, z, norm_w)
    return _matmul(yn, w_out, mode="res", res=xa, gate=gate, ctx_len=ctx_len)


def _cpow(ar, ai, n):
    rr, ri = None, None
    br, bi = ar, ai
    while n:
        if n & 1:
            if rr is None:
                rr, ri = br, bi
            else:
                rr, ri = rr * br - ri * bi, rr * bi + ri * br
        n >>= 1
        if n:
            br, bi = br * br - bi * bi, 2.0 * br * bi
    return rr, ri


def _s5_kernel(u_ref, wb_ref, wc_ref, a_ref, dsk_ref, o_ref, yacc_ref, sbuf_ref, *, ctx_len, tt):
    total = u_ref.shape[0]
    half = wb_ref.shape[-1] // 2
    nseg = S5_SEGMENTS
    tile_rows = nseg * tt
    rowid = lax.broadcasted_iota(jnp.int32, (nseg, half), 0)
    zero = jnp.zeros((nseg, half), F32)
    yacc_ref[...] = jnp.zeros_like(yacc_ref)

    dirs = (0, 1)
    wb = [wb_ref[d, 0] for d in dirs]
    wc = [wc_ref[d, 0] for d in dirs]
    a1 = [(a_ref[d, 0][:, :half], a_ref[d, 0][:, half:]) for d in dirs]
    ab = [(jnp.broadcast_to(r, (nseg, half)), jnp.broadcast_to(i, (nseg, half))) for r, i in a1]
    hcar = [(jnp.zeros((1, half), F32), jnp.zeros((1, half), F32)) for _ in dirs]
    steps = (list(range(tt)), list(reversed(range(tt))))

    def step(d, bu, t, sr, si):
        ar, ai = ab[d]
        br = bu[t * nseg:(t + 1) * nseg, :half]
        bi = bu[t * nseg:(t + 1) * nseg, half:]
        return ar * sr - ai * si + br, ar * si + ai * sr + bi

    for base, n in ((0, ctx_len), (ctx_len, total - ctx_len)):
        ls = n // nseg
        ntile = ls // tt

        def tile_rows_of(d, k, ls=ls, base=base):
            t0 = k * tt if d == 0 else ls - (k + 1) * tt
            return pl.ds(pl.multiple_of(base + t0 * nseg, tile_rows), tile_rows)

        def project_in(d, k, tile_rows_of=tile_rows_of):
            return jnp.dot(u_ref[tile_rows_of(d, k), :].astype(BF16), wb[d], preferred_element_type=F32)

        def pass1(k, carry, project_in=project_in):
            s = list(carry)
            bu = [project_in(d, k) for d in dirs]
            for q in range(tt):
                for d in dirs:
                    s[2 * d], s[2 * d + 1] = step(d, bu[d], steps[d][q], s[2 * d], s[2 * d + 1])
            return tuple(s)

        sf = lax.fori_loop(0, ntile, pass1, (zero,) * 4)

        hin = []
        for d in dirs:
            alr, ali = _cpow(a1[d][0], a1[d][1], ls)
            hr, hi = hcar[d]
            hinr, hini = zero, zero
            for sg in (range(nseg) if d == 0 else reversed(range(nseg))):
                hinr = jnp.where(rowid == sg, hr, hinr)
                hini = jnp.where(rowid == sg, hi, hini)
                hr, hi = (alr * hr - ali * hi + sf[2 * d][sg:sg + 1], alr * hi + ali * hr + sf[2 * d + 1][sg:sg + 1])
            hcar[d] = (hr, hi)
            hin += [hinr, hini]

        def pass2(k, carry, project_in=project_in, tile_rows_of=tile_rows_of):
            s = list(carry)
            bu = [project_in(d, k) for d in dirs]
            for q in range(tt):
                for d in dirs:
                    t = steps[d][q]
                    s[2 * d], s[2 * d + 1] = step(d, bu[d], t, s[2 * d], s[2 * d + 1])
                    sbuf_ref[d, t * nseg:(t + 1) * nseg, :half] = s[2 * d]
                    sbuf_ref[d, t * nseg:(t + 1) * nseg, half:] = s[2 * d + 1]
            for d in dirs:
                y = jnp.dot(sbuf_ref[d].astype(BF16), wc[d], preferred_element_type=F32)
                yacc_ref[tile_rows_of(d, k), :] += y
            return tuple(s)

        lax.fori_loop(0, ntile, pass2, tuple(hin))

    y = yacc_ref[...] + dsk_ref[...] * u_ref[...]
    o_ref[...] = jax.nn.gelu(y).astype(o_ref.dtype)


def _s5_weights(lam_re, lam_im, log_step, b_re, b_im, c_re, c_im):
    step = jnp.exp(log_step.astype(F32))[..., None]
    lr, li = lam_re.astype(F32), lam_im.astype(F32)
    mag = jnp.exp(lr * step)
    ar, ai = mag * jnp.cos(li * step), mag * jnp.sin(li * step)
    den = lr * lr + li * li
    kr = ((ar - 1.0) * lr + ai * li) / den
    ki = (ai * lr - (ar - 1.0) * li) / den
    br, bi = b_re.astype(F32), b_im.astype(F32)
    bbr = kr[..., None] * br - ki[..., None] * bi
    bbi = kr[..., None] * bi + ki[..., None] * br
    cr, ci = c_re.astype(F32), c_im.astype(F32)
    ng, pst, gc = bbr.shape[1:]
    gpb = LANES // gc
    nb = ng // gpb
    eye = jnp.eye(gpb, dtype=F32)

    def in_w(bb):
        bb = bb.reshape(2, nb, gpb, pst, gc)
        return jnp.einsum('dbgpc,gh->dbgchp', bb, eye).reshape(2, nb, gpb * gc, gpb * pst)

    def out_w(cc):
        cc = cc.reshape(2, nb, gpb, gc, pst)
        return jnp.einsum('dbgcp,gh->dbhpgc', cc, eye).reshape(2, nb, gpb * pst, gpb * gc)

    wb = jnp.concatenate([in_w(bbr), in_w(bbi)], axis=-1).astype(BF16)
    wc = jnp.concatenate([out_w(cr), out_w(-ci)], axis=-2).astype(BF16)
    avec = jnp.concatenate([ar.reshape(2, nb, 1, gpb * pst), ai.reshape(2, nb, 1, gpb * pst)], axis=-1)
    return wb, wc, avec


def _seg_permute(x, ctx_len, inverse):
    def perm(v):
        n = v.shape[0]
        shape = (n // S5_SEGMENTS, S5_SEGMENTS) if inverse else (S5_SEGMENTS, n // S5_SEGMENTS)
        return jnp.swapaxes(v.reshape(*shape, -1), 0, 1).reshape(n, -1)
    return jnp.concatenate([perm(x[:ctx_len]), perm(x[ctx_len:])], axis=0)


def _s5_scan(u, wb, wc, avec, d_skip, ctx_len):
    t, d = u.shape
    nb = d // LANES
    feat = wb.shape[-1]
    tt = S5_TIME_TILE
    assert ctx_len % (S5_SEGMENTS * tt) == 0 and (t - ctx_len) % (S5_SEGMENTS * tt) == 0
    return pl.pallas_call(
        functools.partial(_s5_kernel, ctx_len=ctx_len, tt=tt),
        grid=(nb,),
        in_specs=[
            pl.BlockSpec((t, LANES), lambda b: (0, b)),
            pl.BlockSpec((2, 1, LANES, feat), lambda b: (0, b, 0, 0)),
            pl.BlockSpec((2, 1, feat, LANES), lambda b: (0, b, 0, 0)),
            pl.BlockSpec((2, 1, 1, feat), lambda b: (0, b, 0, 0)),
            pl.BlockSpec((1, LANES), lambda b: (0, b)),
        ],
        out_specs=pl.BlockSpec((t, LANES), lambda b: (0, b)),
        out_shape=jax.ShapeDtypeStruct((t, d), BF16),
        scratch_shapes=[pltpu.VMEM((t, LANES), F32), pltpu.VMEM((2, S5_SEGMENTS * tt, feat), F32)],
        compiler_params=_params("parallel"),
    )(u, wb, wc, avec, d_skip.astype(F32).reshape(1, d))


def _s5_mixer(u, xa, gate, lam_re, lam_im, log_step, b_re, b_im, c_re, c_im, d_skip, glu_w, glu_b, ctx_len):
    wb, wc, avec = _s5_weights(lam_re, lam_im, log_step, b_re, b_im, c_re, c_im)
    y = _seg_permute(_s5_scan(_seg_permute(u, ctx_len, False), wb, wc, avec, d_skip, ctx_len), ctx_len, True)
    return _matmul(y, glu_w, mode="glu", bias=glu_b.astype(F32), res=xa, gate=gate, ctx_len=ctx_len)


def _qk_prep_kernel(x_ref, g_ref, cos_ref, sin_ref, o_ref, *, dh, scale):
    cos = cos_ref[...]
    sin = sin_ref[...]
    g = g_ref[...]
    lane = lax.broadcasted_iota(jnp.int32, cos.shape, 1)
    quarter = dh // 4
    first = (lane // quarter) % 2 == 0
    for s in range(x_ref.shape[1] // dh):
        x = x_ref[:, s * dh:(s + 1) * dh]
        y = x * lax.rsqrt(jnp.mean(x * x, axis=-1, keepdims=True) + RMS_EPS) * g
        partner = jnp.where(first, pltpu.roll(y, dh - quarter, axis=1), pltpu.roll(y, quarter, axis=1))
        o_ref[:, s * dh:(s + 1) * dh] = ((y * cos + partner * sin) * scale).astype(o_ref.dtype)


def _qk_prep(x, gain, cos, sin, first_block, nblk, scale):
    t = x.shape[0]
    dh = gain.shape[0]
    width = 4 * dh
    tm = _pick(t, (768, 512, 256))
    return pl.pallas_call(
        functools.partial(_qk_prep_kernel, dh=dh, scale=scale),
        grid=(t // tm, nblk),
        in_specs=[
            pl.BlockSpec((tm, width), lambda i, j: (i, j + first_block)),
            pl.BlockSpec((1, dh), lambda i, j: (0, 0)),
            pl.BlockSpec((tm, dh), lambda i, j: (i, 0)),
            pl.BlockSpec((tm, dh), lambda i, j: (i, 0)),
        ],
        out_specs=pl.BlockSpec((tm, width), lambda i, j: (i, j)),
        out_shape=jax.ShapeDtypeStruct((t, nblk * width), BF16),
        compiler_params=_params("parallel", "parallel"),
    )(x, gain.astype(F32).reshape(1, dh), cos, sin)


def _flash_kernel(q_ref, k_ref, v_ref, lam_ref, g_ref, o_ref, m_ref, l_ref, acc_ref, *, ctx_len, tk, dh,
                  post_scale):
    i = pl.program_id(1)
    total = k_ref.shape[0]
    hw = 2 * dh
    m_ref[...] = jnp.full_like(m_ref, -jnp.inf)
    l_ref[...] = jnp.zeros_like(l_ref)
    acc_ref[...] = jnp.zeros_like(acc_ref)

    def lane_groups(x):
        return [x[:, c * LANES:(c + 1) * LANES] for c in range(x.shape[1] // LANES)]

    def visit(start, size):
        k = k_ref[pl.ds(start, size), :]
        v = v_ref[pl.ds(start, size), :]
        for j in range(2):
            q = q_ref[:, j * dh:(j + 1) * dh]
            s = lax.dot_general(q, k[:, j * dh:(j + 1) * dh], (((1,), (1,)), ((), ())),
                                preferred_element_type=F32)
            group_max = functools.reduce(jnp.maximum, lane_groups(s))
            m_old = m_ref[j]
            m_new = jnp.maximum(m_old, jnp.max(group_max, axis=-1, keepdims=True))
            alpha = jnp.exp2(m_old - m_new)
            p = jnp.exp2(s - jnp.tile(m_new, (1, size // LANES)))
            l_ref[j] = alpha * l_ref[j] + functools.reduce(jnp.add, lane_groups(p))
            acc_ref[j] = (jnp.tile(alpha, (1, hw // LANES)) * acc_ref[j]
                          + jnp.dot(p.astype(BF16), v, preferred_element_type=F32))
            m_ref[j] = m_new

    visit(0, ctx_len)
    nlat = jnp.where(i == 0, 0, (total - ctx_len) // tk)

    def body(c, carry):
        visit(pl.multiple_of(ctx_len + c * tk, tk), tk)
        return carry

    lax.fori_loop(0, nlat, body, 0)
    l0 = jnp.sum(l_ref[0], axis=-1, keepdims=True)
    l1 = jnp.sum(l_ref[1], axis=-1, keepdims=True)
    o = acc_ref[0] / l0 - lam_ref[...] * (acc_ref[1] / l1)
    o = o * lax.rsqrt(jnp.mean(o * o, axis=-1, keepdims=True) + RMS_EPS) * g_ref[...] * post_scale
    o_ref[...] = o.astype(o_ref.dtype)


def _flash(q, k, v, lam, sub_gain, ctx_len, post_scale):
    t, d = q.shape
    dh = sub_gain.shape[0] // 2
    hw = 2 * dh
    tq = ctx_len
    tk = _pick(t - ctx_len, (1024, 512, 256, 128))
    lamv = jnp.full((1, hw), lam, F32)
    return pl.pallas_call(
        functools.partial(_flash_kernel, ctx_len=ctx_len, tk=tk, dh=dh, post_scale=post_scale),
        grid=(d // hw, t // tq),
        in_specs=[
            pl.BlockSpec((tq, hw), lambda h, i: (i, h)),
            pl.BlockSpec((t, hw), lambda h, i: (0, h)),
            pl.BlockSpec((t, hw), lambda h, i: (0, h)),
            pl.BlockSpec((1, hw), lambda h, i: (0, 0)),
            pl.BlockSpec((1, hw), lambda h, i: (0, 0)),
        ],
        out_specs=pl.BlockSpec((tq, hw), lambda h, i: (i, h)),
        out_shape=jax.ShapeDtypeStruct((t, d), BF16),
        scratch_shapes=[pltpu.VMEM((2, tq, LANES), F32), pltpu.VMEM((2, tq, LANES), F32),
                        pltpu.VMEM((2, tq, hw), F32)],
        compiler_params=_params("parallel", "arbitrary"),
    )(q, k, v, lamv, sub_gain.astype(F32).reshape(1, hw))


def _rope_tables(seq, ctx_len, dh):
    rows = seq // GRID_W
    row = jnp.repeat(jnp.arange(rows, dtype=F32), GRID_W)
    col = jnp.tile(jnp.arange(GRID_W, dtype=F32), rows)
    n_freq = dh // 4
    inv_freq = ROPE_BASE ** (-jnp.arange(n_freq, dtype=F32) / n_freq)
    ang_r = row[:, None] * inv_freq
    ang_c = col[:, None] * inv_freq
    cos = jnp.concatenate([jnp.cos(ang_r), jnp.cos(ang_r), jnp.cos(ang_c), jnp.cos(ang_c)], axis=-1)
    sin = jnp.concatenate([-jnp.sin(ang_r), jnp.sin(ang_r), -jnp.sin(ang_c), jnp.sin(ang_c)], axis=-1)
    cos = jnp.concatenate([jnp.ones((ctx_len, dh), F32), cos], axis=0)
    sin = jnp.concatenate([jnp.zeros((ctx_len, dh), F32), sin], axis=0)
    return cos, sin


def _diff_attn_mixer(h, xa, gate, w_q, w_k, w_v, w_o, q_gain, k_gain, lam_q1, lam_k1, lam_q2, lam_k2,
                     sub_gain, lambda_init, ctx_len):
    t, d = h.shape
    dh = q_gain.shape[0]
    v = _matmul(h, w_v, out_dtype=BF16)
    cos, sin = _rope_tables(t - ctx_len, ctx_len, dh)
    nblk = d // (4 * dh)
    q = _qk_prep(_matmul(h, w_q, out_dtype=F32), q_gain, cos, sin, 0, nblk, dh ** -0.5 * math.log2(math.e))
    k = _qk_prep(_matmul(h, w_k, out_dtype=F32), k_gain, cos, sin, 0, nblk, 1.0)
    lam = (jnp.exp(jnp.sum(lam_q1.astype(F32) * lam_k1.astype(F32)))
           - jnp.exp(jnp.sum(lam_q2.astype(F32) * lam_k2.astype(F32))) + lambda_init)
    o = _flash(q, k, v, lam, sub_gain, ctx_len, 1.0 - lambda_init)
    return _matmul(o, w_o, mode="res", res=xa, gate=gate, ctx_len=ctx_len)


def _conv_ffn(hf, xa, gate, w_up, conv_w, conv_b, w_down, ctx_len):
    a = _matmul(hf, w_up, out_dtype=F32)
    tgt = _conv_glu(a, conv_w.astype(F32), conv_b.astype(F32), ctx_len, BF16)
    return _matmul(tgt, w_down, mode="res", res=xa, gate=gate, ctx_len=ctx_len)


def kernel(x, c, ctx, c_ctx, mod_w, mod_b, norm_mix, norm_ffn, ffn_up, ffn_conv_w, ffn_conv_b, ffn_down, ssd_w_in, ssd_conv_w, ssd_conv_b, ssd_a_log, ssd_dt_bias, ssd_d, ssd_norm, ssd_w_out, s5_lam_re, s5_lam_im, s5_log_step, s5_b_re, s5_b_im, s5_c_re, s5_c_im, s5_d, s5_glu_w, s5_glu_b, da_w_q, da_w_k, da_w_v, da_w_o, da_q_norm, da_k_norm, da_lam_q1, da_lam_k1, da_lam_q2, da_lam_k2, da_sub_norm):
    bsz, seq, d = x.shape
    assert bsz == 1
    ctx_len = ctx.shape[1]
    depth = mod_w.shape[0]
    cond = jnp.concatenate([jax.nn.silu(c.astype(F32)), jax.nn.silu(c_ctx.astype(F32))[None, :]], axis=0)
    mod = _modulation(cond, mod_w, mod_b).reshape(depth, 2, 6, d)
    xa = jnp.concatenate([ctx[0], x[0]], axis=0).astype(F32)

    for i in range(depth):
        kind, j = i % N_MIXERS, i // N_MIXERS
        m = mod[i]
        g1, g2 = m[:, 2], m[:, 5]
        h = _norm_mod(xa, norm_mix[i], m, 0, 1, ctx_len, F32 if kind == 1 else BF16)
        if kind == 0:
            xa = _ssd_mixer(h, xa, g1, ssd_w_in[j], ssd_conv_w[j], ssd_conv_b[j], ssd_a_log[j], ssd_dt_bias[j],
                            ssd_d[j], ssd_norm[j], ssd_w_out[j], ctx_len)
        elif kind == 1:
            xa = _s5_mixer(h, xa, g1, s5_lam_re[j], s5_lam_im[j], s5_log_step[j], s5_b_re[j], s5_b_im[j],
                           s5_c_re[j], s5_c_im[j], s5_d[j], s5_glu_w[j], s5_glu_b[j], ctx_len)
        else:
            lambda_init = 0.8 - 0.6 * math.exp(-0.3 * i)
            xa = _diff_attn_mixer(h, xa, g1, da_w_q[j], da_w_k[j], da_w_v[j], da_w_o[j], da_q_norm[j],
                                  da_k_norm[j], da_lam_q1[j], da_lam_k1[j], da_lam_q2[j], da_lam_k2[j],
                                  da_sub_norm[j], lambda_init, ctx_len)
        hf = _norm_mod(xa, norm_ffn[i], m, 3, 4, ctx_len, BF16)
        xa = _conv_ffn(hf, xa, g2, ffn_up[i], ffn_conv_w[i], ffn_conv_b[i], ffn_down[i], ctx_len)
    return xa[ctx_len:][None]
```

```python
import functools
import math

import jax
import jax.numpy as jnp
from jax import lax
from jax.experimental import pallas as pl
from jax.experimental.pallas import tpu as pltpu

F32 = jnp.float32
BF16 = jnp.bfloat16

RMS_EPS = 1e-6
ROPE_BASE = 10000.0
GRID_W = 64
N_MIXERS = 3

LANES = 128
SUBLANES = 8
VMEM_LIMIT_BYTES = 52 * 1024 * 1024

SSD_GROUPS = 8
SSD_HEAD_DIM = 64
SSD_STATE = 128
SSD_CHUNK = 128
S5_SEGMENTS = SUBLANES
S5_TIME_TILE = 32


def _params(*semantics):
    return pltpu.CompilerParams(dimension_semantics=semantics, vmem_limit_bytes=VMEM_LIMIT_BYTES)


def _pick(n, candidates):
    for c in candidates:
        if n % c == 0:
            return c
    raise ValueError(f"no tile in {candidates} divides {n}")


def _sigmoid(x):
    return 1.0 / (1.0 + jnp.exp(-x))


def _silu(x):
    return x * _sigmoid(x)


def _mod_kernel(c_ref, w_ref, b_ref, o_ref, acc_ref, *, nk):
    k = pl.program_id(2)

    @pl.when(k == 0)
    def _():
        acc_ref[...] = jnp.zeros_like(acc_ref)

    w = w_ref[0]
    tk, tn = w.shape
    for r in range(2):
        cb = jnp.tile(c_ref[r], (1, tn // LANES))
        acc_ref[r] += (w * cb).reshape(tk // SUBLANES, SUBLANES, tn).sum(axis=0)

    @pl.when(k == nk - 1)
    def _():
        o_ref[0] = acc_ref[...].sum(axis=1) + b_ref[0]


def _modulation(cond, mod_w, mod_b):
    depth, d, n = mod_w.shape
    tk = _pick(d, (512, 256, 128))
    tn = _pick(n, (2048, 1024, 512))
    nk = d // tk
    condb = jnp.broadcast_to(cond[:, :, None], (2, d, LANES))
    return pl.pallas_call(
        functools.partial(_mod_kernel, nk=nk),
        grid=(depth, n // tn, nk),
        in_specs=[
            pl.BlockSpec((2, tk, LANES), lambda l, j, k: (0, k, 0)),
            pl.BlockSpec((1, tk, tn), lambda l, j, k: (l, k, j)),
            pl.BlockSpec((1, 1, tn), lambda l, j, k: (l, 0, j)),
        ],
        out_specs=pl.BlockSpec((1, 2, tn), lambda l, j, k: (l, 0, j)),
        out_shape=jax.ShapeDtypeStruct((depth, 2, n), F32),
        scratch_shapes=[pltpu.VMEM((2, SUBLANES, tn), F32)],
        compiler_params=_params("parallel", "parallel", "arbitrary"),
    )(condb, mod_w, mod_b.reshape(depth, 1, n))


def _norm_mod_kernel(x_ref, g_ref, m_ref, o_ref, *, shift_row, scale_row):
    x = x_ref[...]
    y = x * lax.rsqrt(jnp.mean(x * x, axis=-1, keepdims=True) + RMS_EPS) * g_ref[...]
    m = m_ref[0]
    o_ref[...] = (y * (1.0 + m[scale_row:scale_row + 1]) + m[shift_row:shift_row + 1]).astype(o_ref.dtype)


def _norm_mod(xa, gain, mod, shift_row, scale_row, ctx_len, out_dtype):
    t, d = xa.shape
    tm = _pick(ctx_len, (256, 128))
    ncb = ctx_len // tm
    return pl.pallas_call(
        functools.partial(_norm_mod_kernel, shift_row=shift_row, scale_row=scale_row),
        grid=(t // tm,),
        in_specs=[
            pl.BlockSpec((tm, d), lambda i: (i, 0)),
            pl.BlockSpec((1, d), lambda i: (0, 0)),
            pl.BlockSpec((1, 6, d), lambda i: (jnp.where(i < ncb, 1, 0), 0, 0)),
        ],
        out_specs=pl.BlockSpec((tm, d), lambda i: (i, 0)),
        out_shape=jax.ShapeDtypeStruct((t, d), out_dtype),
        compiler_params=_params("parallel"),
    )(xa, gain.reshape(1, d), mod)


def _gate_rows(gate_ref, i, tm, tn, ctx_len):
    rows = i * tm + lax.broadcasted_iota(jnp.int32, (tm, tn), 0)
    return jnp.where(rows < ctx_len, gate_ref[1:2, :], gate_ref[0:1, :])


def _mm_kernel(*refs, nk, mode, ctx_len, cast_w):
    i = pl.program_id(1)
    k = pl.program_id(2)
    nw = 2 if mode == "glu" else 1
    a_ref, w_refs, rest = refs[0], refs[1:1 + nw], refs[1 + nw:]
    if mode == "plain":
        o_ref, scratch = rest[0], rest[1:]
    elif mode == "res":
        (res_ref, gate_ref, o_ref), scratch = rest[:3], rest[3:]
    else:
        (ba_ref, bb_ref, res_ref, gate_ref, o_ref), scratch = rest[:5], rest[5:]

    if cast_w:
        wbf, scratch = scratch[:nw], scratch[nw:]

        @pl.when(i == 0)
        def _():
            for dst, src in zip(wbf, w_refs):
                dst[...] = src[...].astype(BF16)

        w_refs = wbf
    accs = scratch

    a = a_ref[...]
    prods = [jnp.dot(a, w[...], preferred_element_type=F32) for w in w_refs]

    def finish(vals):
        tm, tn = vals[0].shape
        if mode == "plain":
            o_ref[...] = vals[0].astype(o_ref.dtype)
        elif mode == "res":
            o_ref[...] = res_ref[...] + _gate_rows(gate_ref, i, tm, tn, ctx_len) * vals[0]
        else:
            o = (vals[0] + ba_ref[...]) * _sigmoid(vals[1] + bb_ref[...])
            o_ref[...] = res_ref[...] + _gate_rows(gate_ref, i, tm, tn, ctx_len) * o

    if nk == 1:
        finish(prods)
    else:
        @pl.when(k == 0)
        def _():
            for acc, p in zip(accs, prods):
                acc[...] = p

        @pl.when(k > 0)
        def _():
            for acc, p in zip(accs, prods):
                acc[...] += p

        @pl.when(k == nk - 1)
        def _():
            finish([acc[...] for acc in accs])


def _weight_spec(w, layer, tk, tn, col_block):
    if w.ndim == 3:
        return pl.BlockSpec((None, tk, tn), lambda j, i, k: (layer, k, j + col_block))
    return pl.BlockSpec((tk, tn), lambda j, i, k: (k, j + col_block))


def _matmul(a, w, *, layer=0, n=None, col_off=0, out_dtype=F32, mode="plain", res=None, gate=None, bias=None,
            ctx_len=0):
    m, kdim = a.shape
    n_w = w.shape[-1]
    nw = 2 if mode == "glu" else 1
    if n is None:
        n = n_w // nw
    tm = _pick(m, (768, 512, 256))
    tk = _pick(kdim, (4096, 2048, 1024, 512))
    nk = kdim // tk
    cast_w = w.dtype == F32 and nk == 1
    if w.dtype == F32 and not cast_w:
        w = (w[layer] if w.ndim == 3 else w).astype(BF16)
    tn = n if n < LANES else _pick(n, (256, 128) if (cast_w and nw == 2) else (512, 256, 128))
    assert col_off % tn == 0
    nb, cb = n // tn, col_off // tn
    a_spec = pl.BlockSpec((tm, tk), lambda j, i, k: (i, k))
    o_spec = pl.BlockSpec((tm, tn), lambda j, i, k: (i, j))
    w_spec = _weight_spec(w, layer, tk, tn, cb)
    row_spec = pl.BlockSpec((1, tn), lambda j, i, k: (0, j))
    gate_spec = pl.BlockSpec((2, tn), lambda j, i, k: (0, j))
    if mode == "plain":
        in_specs, args = [a_spec, w_spec], (a, w)
    elif mode == "res":
        in_specs, args = [a_spec, w_spec, o_spec, gate_spec], (a, w, res, gate)
    else:
        wb_spec = _weight_spec(w, layer, tk, tn, nb)
        bb_spec = pl.BlockSpec((1, tn), lambda j, i, k: (0, j + nb))
        b2 = bias.reshape(1, n_w)
        in_specs = [a_spec, w_spec, wb_spec, row_spec, bb_spec, o_spec, gate_spec]
        args = (a, w, w, b2, b2, res, gate)
    scratch = [pltpu.VMEM((tk, tn), BF16)] * nw if cast_w else []
    scratch += [pltpu.VMEM((tm, tn), F32)] * nw if nk > 1 else []
    semantics = ("arbitrary",) * 3 if cast_w else ("parallel", "parallel", "arbitrary")
    return pl.pallas_call(
        functools.partial(_mm_kernel, nk=nk, mode=mode, ctx_len=ctx_len, cast_w=cast_w),
        grid=(nb, m // tm, nk),
        in_specs=in_specs,
        out_specs=o_spec,
        out_shape=jax.ShapeDtypeStruct((m, n), out_dtype),
        scratch_shapes=scratch,
        compiler_params=_params(*semantics),
    )(*args)


CONV_HALO = 16


def _mm_conv_kernel(*refs, glu, ctx_len, total, tm, cast_w):
    i = pl.program_id(1)
    nw = 2 if glu else 1
    a_ref, ap_ref, an_ref = refs[:3]
    w_refs = refs[3:3 + nw]
    cw_refs = refs[3 + nw:3 + 2 * nw]
    cb_refs = refs[3 + 2 * nw:3 + 3 * nw]
    o_ref = refs[3 + 3 * nw]
    aext_ref = refs[4 + 3 * nw]
    if cast_w:
        wbf = refs[5 + 3 * nw:5 + 4 * nw]

        @pl.when(i == 0)
        def _():
            for dst, src in zip(wbf, w_refs):
                dst[...] = src[...].astype(BF16)

        w_refs = wbf

    hb = CONV_HALO
    aext_ref[0:hb, :] = ap_ref[...]
    aext_ref[hb:hb + tm, :] = a_ref[...]
    aext_ref[hb + tm:, :] = an_ref[...]
    a = aext_ref[...]
    tn = o_ref.shape[1]
    accs = [jnp.dot(a, w_ref[...], preferred_element_type=F32) for w_ref in w_refs]

    def conv(masked):
        row = i * tm + lax.broadcasted_iota(jnp.int32, (tm, LANES), 0)
        in_ctx = row < ctx_len
        outs = []
        for acc, cw_ref, cb_ref in zip(accs, cw_refs, cb_refs):
            taps = cw_ref.shape[0]
            rad = taps // 2
            out = cb_ref[...] + acc[hb:hb + tm] * cw_ref[rad:rad + 1, :]
            for j in range(taps):
                s = j - rad
                if s == 0:
                    continue
                shifted = pltpu.roll(acc, (-s) % acc.shape[0], axis=0)[hb:hb + tm]
                if masked:
                    src = row + s
                    same_seq = jnp.logical_not(jnp.logical_xor(in_ctx, src < ctx_len))
                    ok = jnp.logical_and(jnp.logical_and(src >= 0, src < total), same_seq)
                    shifted = shifted * jnp.tile(jnp.where(ok, 1.0, 0.0), (1, tn // LANES))
                out = out + shifted * cw_ref[j:j + 1, :]
            outs.append(out)
        o = _silu(outs[0]) * outs[1] if glu else _silu(outs[0])
        o_ref[...] = o.astype(o_ref.dtype)

    start = i * tm
    near_edge = jnp.logical_or(jnp.logical_or(start == 0, start + tm >= total),
                               jnp.logical_and(start < ctx_len + hb, start + tm + hb > ctx_len))
    pl.when(near_edge)(lambda: conv(True))
    pl.when(jnp.logical_not(near_edge))(lambda: conv(False))


def _matmul_conv(a, w, conv_w, conv_b, *, layer=0, n=None, col_off=0, glu=False, ctx_len=0, out_dtype=BF16):
    m, kdim = a.shape
    nw = 2 if glu else 1
    if n is None:
        n = w.shape[-1] // nw
    tm = _pick(m, (768, 512, 256))
    cast_w = w.dtype == F32
    tn = _pick(n, (256, 128) if (cast_w and glu) else (512, 256, 128))
    assert col_off % tn == 0 and tm % CONV_HALO == 0 and m % CONV_HALO == 0
    nb, cb = n // tn, col_off // tn
    hb = CONV_HALO
    rb, nhb = tm // hb, m // hb
    taps = conv_w.shape[0]
    a_spec = pl.BlockSpec((tm, kdim), lambda j, i, k: (i, 0))
    ap_spec = pl.BlockSpec((hb, kdim), lambda j, i, k: (jnp.maximum(i * rb - 1, 0), 0))
    an_spec = pl.BlockSpec((hb, kdim), lambda j, i, k: (jnp.minimum((i + 1) * rb, nhb - 1), 0))
    cb2 = conv_b.astype(F32).reshape(1, -1)
    cw2 = conv_w.astype(F32)
    in_specs = [a_spec, ap_spec, an_spec]
    in_specs += [_weight_spec(w, layer, kdim, tn, cb + h * nb) for h in range(nw)]
    in_specs += [pl.BlockSpec((taps, tn), lambda j, i, k, h=h: (0, j + h * nb)) for h in range(nw)]
    in_specs += [pl.BlockSpec((1, tn), lambda j, i, k, h=h: (0, j + h * nb)) for h in range(nw)]
    scratch = [pltpu.VMEM((tm + 2 * hb, kdim), BF16)]
    scratch += [pltpu.VMEM((kdim, tn), BF16)] * nw if cast_w else []
    return pl.pallas_call(
        functools.partial(_mm_conv_kernel, glu=glu, ctx_len=ctx_len, total=m, tm=tm, cast_w=cast_w),
        grid=(nb, m // tm, 1),
        in_specs=in_specs,
        out_specs=pl.BlockSpec((tm, tn), lambda j, i, k: (i, j)),
        out_shape=jax.ShapeDtypeStruct((m, n), out_dtype),
        scratch_shapes=scratch,
        compiler_params=_params("arbitrary", "arbitrary", "arbitrary"),
    )(a, a, a, *([w] * nw), *([cw2] * nw), *([cb2] * nw))


def _split3(v):
    h1 = v.astype(BF16)
    r1 = v - h1.astype(F32)
    h2 = r1.astype(BF16)
    h3 = (r1 - h2.astype(F32)).astype(BF16)
    return h1, h2, h3


def _ssd_dt_kernel(raw_ref, bias_ref, a_ref, dt_ref, cum_ref, tot_ref, *, nfwd):
    x = raw_ref[...] + bias_ref[...]
    dt = jnp.maximum(x, 0.0) + jnp.log(1.0 + jnp.exp(-jnp.abs(x)))
    dt_ref[...] = dt
    da = dt * a_ref[...]
    l = da.shape[0]
    row = lax.broadcasted_iota(jnp.int32, (l, l), 0)
    col = lax.broadcasted_iota(jnp.int32, (l, l), 1)
    incl_m = jnp.where(row >= col, 1.0, 0.0).astype(BF16)
    excl_m = jnp.where(row > col, 1.0, 0.0).astype(BF16)
    parts = _split3(da)
    incl = sum(jnp.dot(incl_m, p, preferred_element_type=F32) for p in parts)
    excl = sum(jnp.dot(excl_m, p, preferred_element_type=F32) for p in parts)
    lane = lax.broadcasted_iota(jnp.int32, da.shape, 1)
    cum_ref[...] = jnp.where(lane < nfwd, incl, excl)
    tot_ref[0] = incl[l - 1:l, :]


def _ssd_dt(dt_raw, dt_bias, a_neg):
    t, c = dt_raw.shape
    nc = t // SSD_CHUNK
    row = pl.BlockSpec((1, c), lambda i: (0, 0))
    blk = pl.BlockSpec((SSD_CHUNK, c), lambda i: (i, 0))
    return pl.pallas_call(
        functools.partial(_ssd_dt_kernel, nfwd=c // 2),
        grid=(nc,),
        in_specs=[blk, row, row],
        out_specs=[blk, blk, pl.BlockSpec((1, 1, c), lambda i: (i, 0, 0))],
        out_shape=[jax.ShapeDtypeStruct((t, c), F32), jax.ShapeDtypeStruct((t, c), F32),
                   jax.ShapeDtypeStruct((nc, 1, c), F32)],
        compiler_params=_params("parallel"),
    )(dt_raw, dt_bias.reshape(1, c), a_neg.reshape(1, c))


def _ssd_scan_kernel(xf_ref, bf_ref, cf_ref, pf_ref, ptf_ref, totf_ref, tef_ref,
                     xb_ref, bb_ref, cb_ref, pb_ref, totb_ref, teb_ref,
                     e_ref, dsk_ref, yf_ref, yb_ref, h_ref, *, hpg):
    pdim = SSD_HEAD_DIM
    j = pl.program_id(1)

    @pl.when(j == 0)
    def _():
        h_ref[...] = jnp.zeros_like(h_ref)

    e = e_ref[...]

    def expand(v):
        return jnp.dot(v.astype(BF16), e, preferred_element_type=F32)

    def state_update(d, x32, bm, scale, total_decay):
        xw = (x32 * expand(scale)).astype(BF16)
        inc = lax.dot_general(bm, xw, (((0,), (0,)), ((), ())), preferred_element_type=F32)
        h_ref[d] = h_ref[d] * total_decay + inc

    x = xf_ref[...]
    x32 = x.astype(F32)
    bm = bf_ref[...]
    cm = cf_ref[...]
    p = pf_ref[0]
    pt = ptf_ref[0]
    dtf, cumf = p[:, 0:hpg], p[:, hpg:2 * hpg]
    cumb = p[:, 3 * hpg:4 * hpg]
    totf = totf_ref[0, 0][:, 0:hpg]
    l = x.shape[0]
    cb = lax.dot_general(cm, bm, (((1,), (1,)), ((), ())), preferred_element_type=F32)
    li = lax.broadcasted_iota(jnp.int32, (l, l), 0)
    si = lax.broadcasted_iota(jnp.int32, (l, l), 1)
    lower = li >= si
    upper = si >= li
    lane = lax.broadcasted_iota(jnp.int32, (l, 2 * pdim), 1)
    neg = jnp.float32(-jnp.inf)
    cm32 = cm.astype(F32)
    hf = h_ref[0].astype(BF16)
    ys = []
    for q in range(hpg // 2):
        rhs = jnp.concatenate([x[:, q * 2 * pdim:(q + 1) * 2 * pdim], hf[:, q * 2 * pdim:(q + 1) * 2 * pdim]],
                              axis=0)
        rs = []
        for h in (2 * q, 2 * q + 1):
            cf_col = jnp.broadcast_to(cumf[:, h:h + 1], (l, l))
            dfw = jnp.exp(jnp.where(lower, cf_col - pt[hpg + h:hpg + h + 1, :], neg)) * pt[h:h + 1, :]
            dbw = (jnp.exp(jnp.where(upper, pt[3 * hpg + h:3 * hpg + h + 1, :] - cumb[:, h:h + 1], neg))
                   * pt[2 * hpg + h:2 * hpg + h + 1, :])
            lhs = jnp.concatenate([(cb * (dfw + dbw)).astype(BF16), (cm32 * jnp.exp(cf_col)).astype(BF16)],
                                  axis=1)
            rs.append(jnp.dot(lhs, rhs, preferred_element_type=F32))
        ys.append(jnp.where(lane < pdim, rs[0], rs[1]))
    yf_ref[...] = jnp.concatenate(ys, axis=1) + dsk_ref[...] * x32
    state_update(0, x32, bm, dtf * jnp.exp(totf - cumf), tef_ref[0, 0])

    xb32 = xb_ref[...].astype(F32)
    pb = pb_ref[0]
    dtb2, cumb2 = pb[:, 2 * hpg:3 * hpg], pb[:, 3 * hpg:4 * hpg]
    totb = totb_ref[0, 0][:, hpg:2 * hpg]
    yb_ref[...] = (jnp.dot(cb_ref[...], h_ref[1].astype(BF16), preferred_element_type=F32)
                   * expand(jnp.exp(totb - cumb2)))
    state_update(1, xb32, bb_ref[...], dtb2 * jnp.exp(cumb2), teb_ref[0, 0])


def _ssd_scan(xbc, dt, cum, tot, d_skip, ctx_len):
    t = xbc.shape[0]
    g, n, pdim, lc = SSD_GROUPS, SSD_STATE, SSD_HEAD_DIM, SSD_CHUNK
    heads = dt.shape[1] // 2
    hpg = heads // g
    gw = hpg * pdim
    nc, ncc = t // lc, ctx_len // lc
    assert (g * gw) % n == 0
    b_off, c_off = g * gw // n, g * gw // n + g

    dt4 = dt.reshape(t, 2, g, hpg)
    cum4 = cum.reshape(t, 2, g, hpg)
    pk = jnp.concatenate([dt4[:, 0], cum4[:, 0], dt4[:, 1], cum4[:, 1]], axis=-1)
    pk = jnp.transpose(pk, (1, 0, 2))
    pkt = jnp.transpose(pk, (0, 2, 1))
    tot4 = tot.reshape(nc, 2, g, hpg)
    totg = jnp.transpose(jnp.concatenate([tot4[:, 0], tot4[:, 1]], axis=-1), (1, 0, 2))
    totg = totg.reshape(g, nc, 1, 2 * hpg)
    tote = jnp.repeat(jnp.exp(tot4), pdim, axis=-1).reshape(nc, 2, 1, g * gw)
    e = jnp.repeat(jnp.eye(hpg, dtype=BF16), pdim, axis=1)
    dsk = jnp.repeat(d_skip.astype(F32), pdim).reshape(1, g * gw)

    def bwd_chunk(j):
        return jnp.where(j < ncc, ncc - 1 - j, nc - 1 - (j - ncc))

    def fwd_chunk(j):
        return j

    in_specs = []
    for direction, chunk in enumerate((fwd_chunk, bwd_chunk)):
        in_specs += [
            pl.BlockSpec((lc, gw), lambda gi, j, c=chunk: (c(j), gi)),
            pl.BlockSpec((lc, n), lambda gi, j, c=chunk: (c(j), b_off + gi)),
            pl.BlockSpec((lc, n), lambda gi, j, c=chunk: (c(j), c_off + gi)),
            pl.BlockSpec((1, lc, 4 * hpg), lambda gi, j, c=chunk: (gi, c(j), 0)),
        ]
        if direction == 0:
            in_specs.append(pl.BlockSpec((1, 4 * hpg, lc), lambda gi, j: (gi, 0, j)))
        in_specs += [
            pl.BlockSpec((1, 1, 1, 2 * hpg), lambda gi, j, c=chunk: (gi, c(j), 0, 0)),
            pl.BlockSpec((1, 1, 1, gw), lambda gi, j, c=chunk, dr=direction: (c(j), dr, 0, gi)),
        ]
    in_specs += [pl.BlockSpec((hpg, gw), lambda gi, j: (0, 0)),
                 pl.BlockSpec((1, gw), lambda gi, j: (0, gi))]
    out_specs = [pl.BlockSpec((lc, gw), lambda gi, j: (j, gi)),
                 pl.BlockSpec((lc, gw), lambda gi, j: (bwd_chunk(j), gi))]
    return pl.pallas_call(
        functools.partial(_ssd_scan_kernel, hpg=hpg),
        grid=(g, nc),
        in_specs=in_specs,
        out_specs=out_specs,
        out_shape=[jax.ShapeDtypeStruct((t, g * gw), F32)] * 2,
        scratch_shapes=[pltpu.VMEM((2, n, gw), F32)],
        compiler_params=_params("parallel", "arbitrary"),
    )(xbc, xbc, xbc, pk, pkt, totg, tote,
      xbc, xbc, xbc, pk, totg, tote, e, dsk)


def _ssd_finish_kernel(yf_ref, yb_ref, z_ref, w_ref, o_ref):
    y = (yf_ref[...] + yb_ref[...]) * _silu(z_ref[...].astype(F32))
    y = y * lax.rsqrt(jnp.mean(y * y, axis=-1, keepdims=True) + RMS_EPS)
    o_ref[...] = (y * w_ref[...]).astype(o_ref.dtype)


def _ssd_finish(yf, yb, z, norm_w):
    t, c = yf.shape
    gw = c // SSD_GROUPS
    tm = _pick(t, (256, 128))
    blk = pl.BlockSpec((tm, gw), lambda i, gi: (i, gi))
    return pl.pallas_call(
        _ssd_finish_kernel,
        grid=(t // tm, SSD_GROUPS),
        in_specs=[blk, blk, blk, pl.BlockSpec((1, gw), lambda i, gi: (0, gi))],
        out_specs=blk,
        out_shape=jax.ShapeDtypeStruct((t, c), BF16),
        compiler_params=_params("parallel", "parallel"),
    )(yf, yb, z, norm_w.reshape(1, c))


def _ssd_mixer(h, xa, gate, layer, w_in, conv_w, conv_b, a_log, dt_bias, d_skip, norm_w, w_out, ctx_len):
    d_inner = w_out.shape[1]
    conv_ch = conv_w.shape[1]
    n_dt = w_in.shape[2] - d_inner - conv_ch
    z = _matmul(h, w_in, layer=layer, n=d_inner, col_off=0, out_dtype=BF16)
    xbc = _matmul_conv(h, w_in, conv_w, conv_b, layer=layer, n=conv_ch, col_off=d_inner, ctx_len=ctx_len)
    dt_raw = _matmul(h, w_in, layer=layer, n=n_dt, col_off=d_inner + conv_ch, out_dtype=F32)
    a_neg = -jnp.exp(a_log.astype(F32)).reshape(-1)
    dt, cum, tot = _ssd_dt(dt_raw, dt_bias.astype(F32).reshape(-1), a_neg)
    yf, yb = _ssd_scan(xbc, dt, cum, tot, d_skip, ctx_len)
    yn = _ssd_finish(yf, yb, z, norm_w)
    return _matmul(yn, w_out, layer=layer, mode="res", res=xa, gate=gate, ctx_len=ctx_len)


def _cpow(ar, ai, n):
    rr, ri = None, None
    br, bi = ar, ai
    while n:
        if n & 1:
            if rr is None:
                rr, ri = br, bi
            else:
                rr, ri = rr * br - ri * bi, rr * bi + ri * br
        n >>= 1
        if n:
            br, bi = br * br - bi * bi, 2.0 * br * bi
    return rr, ri


def _s5_kernel(u_ref, wb_ref, wc_ref, a_ref, dsk_ref, o_ref, yacc_ref, sbuf_ref, *, ctx_len, tt):
    total = u_ref.shape[0]
    half = wb_ref.shape[-1] // 2
    nseg = S5_SEGMENTS
    tile_rows = nseg * tt
    rowid = lax.broadcasted_iota(jnp.int32, (nseg, half), 0)
    zero = jnp.zeros((nseg, half), F32)
    yacc_ref[...] = jnp.zeros_like(yacc_ref)

    dirs = (0, 1)
    wb = [wb_ref[d, 0] for d in dirs]
    wc = [wc_ref[d, 0] for d in dirs]
    a1 = [(a_ref[d, 0][:, :half], a_ref[d, 0][:, half:]) for d in dirs]
    ab = [(jnp.broadcast_to(r, (nseg, half)), jnp.broadcast_to(i, (nseg, half))) for r, i in a1]
    hcar = [(jnp.zeros((1, half), F32), jnp.zeros((1, half), F32)) for _ in dirs]
    steps = (list(range(tt)), list(reversed(range(tt))))

    def step(d, bu, t, sr, si):
        ar, ai = ab[d]
        br = bu[t * nseg:(t + 1) * nseg, :half]
        bi = bu[t * nseg:(t + 1) * nseg, half:]
        return ar * sr - ai * si + br, ar * si + ai * sr + bi

    for base, n in ((0, ctx_len), (ctx_len, total - ctx_len)):
        ls = n // nseg
        ntile = ls // tt

        def tile_rows_of(d, k, ls=ls, base=base):
            t0 = k * tt if d == 0 else ls - (k + 1) * tt
            return pl.ds(pl.multiple_of(base + t0 * nseg, tile_rows), tile_rows)

        def project_in(d, k, tile_rows_of=tile_rows_of):
            return jnp.dot(u_ref[tile_rows_of(d, k), :].astype(BF16), wb[d], preferred_element_type=F32)

        def pass1(k, carry, project_in=project_in):
            s = list(carry)
            bu = [project_in(d, k) for d in dirs]
            for q in range(tt):
                for d in dirs:
                    s[2 * d], s[2 * d + 1] = step(d, bu[d], steps[d][q], s[2 * d], s[2 * d + 1])
            return tuple(s)

        sf = lax.fori_loop(0, ntile, pass1, (zero,) * 4)

        hin = []
        for d in dirs:
            alr, ali = _cpow(a1[d][0], a1[d][1], ls)
            hr, hi = hcar[d]
            hinr, hini = zero, zero
            for sg in (range(nseg) if d == 0 else reversed(range(nseg))):
                hinr = jnp.where(rowid == sg, hr, hinr)
                hini = jnp.where(rowid == sg, hi, hini)
                hr, hi = (alr * hr - ali * hi + sf[2 * d][sg:sg + 1], alr * hi + ali * hr + sf[2 * d + 1][sg:sg + 1])
            hcar[d] = (hr, hi)
            hin += [hinr, hini]

        def pass2(k, carry, project_in=project_in, tile_rows_of=tile_rows_of):
            s = list(carry)
            bu = [project_in(d, k) for d in dirs]
            for q in range(tt):
                for d in dirs:
                    t = steps[d][q]
                    s[2 * d], s[2 * d + 1] = step(d, bu[d], t, s[2 * d], s[2 * d + 1])
                    sbuf_ref[d, t * nseg:(t + 1) * nseg, :half] = s[2 * d]
                    sbuf_ref[d, t * nseg:(t + 1) * nseg, half:] = s[2 * d + 1]
            for d in dirs:
                y = jnp.dot(sbuf_ref[d].astype(BF16), wc[d], preferred_element_type=F32)
                yacc_ref[tile_rows_of(d, k), :] += y
            return tuple(s)

        lax.fori_loop(0, ntile, pass2, tuple(hin))

    y = yacc_ref[...] + dsk_ref[...] * u_ref[...]
    o_ref[...] = jax.nn.gelu(y).astype(o_ref.dtype)


def _s5_weights(lam_re, lam_im, log_step, b_re, b_im, c_re, c_im):
    step = jnp.exp(log_step.astype(F32))[..., None]
    lr, li = lam_re.astype(F32), lam_im.astype(F32)
    mag = jnp.exp(lr * step)
    ar, ai = mag * jnp.cos(li * step), mag * jnp.sin(li * step)
    den = lr * lr + li * li
    kr = ((ar - 1.0) * lr + ai * li) / den
    ki = (ai * lr - (ar - 1.0) * li) / den
    br, bi = b_re.astype(F32), b_im.astype(F32)
    bbr = kr[..., None] * br - ki[..., None] * bi
    bbi = kr[..., None] * bi + ki[..., None] * br
    cr, ci = c_re.astype(F32), c_im.astype(F32)
    ng, pst, gc = bbr.shape[1:]
    gpb = LANES // gc
    nb = ng // gpb
    eye = jnp.eye(gpb, dtype=F32)

    def in_w(bb):
        bb = bb.reshape(2, nb, gpb, pst, gc)
        return jnp.einsum('dbgpc,gh->dbgchp', bb, eye).reshape(2, nb, gpb * gc, gpb * pst)

    def out_w(cc):
        cc = cc.reshape(2, nb, gpb, gc, pst)
        return jnp.einsum('dbgcp,gh->dbhpgc', cc, eye).reshape(2, nb, gpb * pst, gpb * gc)

    wb = jnp.concatenate([in_w(bbr), in_w(bbi)], axis=-1).astype(BF16)
    wc = jnp.concatenate([out_w(cr), out_w(-ci)], axis=-2).astype(BF16)
    avec = jnp.concatenate([ar.reshape(2, nb, 1, gpb * pst), ai.reshape(2, nb, 1, gpb * pst)], axis=-1)
    return wb, wc, avec


def _seg_permute(x, ctx_len, inverse):
    def perm(v):
        n = v.shape[0]
        shape = (n // S5_SEGMENTS, S5_SEGMENTS) if inverse else (S5_SEGMENTS, n // S5_SEGMENTS)
        return jnp.swapaxes(v.reshape(*shape, -1), 0, 1).reshape(n, -1)
    return jnp.concatenate([perm(x[:ctx_len]), perm(x[ctx_len:])], axis=0)


def _s5_scan(u, wb, wc, avec, d_skip, ctx_len):
    t, d = u.shape
    nb = d // LANES
    feat = wb.shape[-1]
    tt = S5_TIME_TILE
    assert ctx_len % (S5_SEGMENTS * tt) == 0 and (t - ctx_len) % (S5_SEGMENTS * tt) == 0
    return pl.pallas_call(
        functools.partial(_s5_kernel, ctx_len=ctx_len, tt=tt),
        grid=(nb,),
        in_specs=[
            pl.BlockSpec((t, LANES), lambda b: (0, b)),
            pl.BlockSpec((2, 1, LANES, feat), lambda b: (0, b, 0, 0)),
            pl.BlockSpec((2, 1, feat, LANES), lambda b: (0, b, 0, 0)),
            pl.BlockSpec((2, 1, 1, feat), lambda b: (0, b, 0, 0)),
            pl.BlockSpec((1, LANES), lambda b: (0, b)),
        ],
        out_specs=pl.BlockSpec((t, LANES), lambda b: (0, b)),
        out_shape=jax.ShapeDtypeStruct((t, d), BF16),
        scratch_shapes=[pltpu.VMEM((t, LANES), F32), pltpu.VMEM((2, S5_SEGMENTS * tt, feat), F32)],
        compiler_params=_params("parallel"),
    )(u, wb, wc, avec, d_skip.astype(F32).reshape(1, d))


def _s5_mixer(u, xa, gate, layer, lam_re, lam_im, log_step, b_re, b_im, c_re, c_im, d_skip, glu_w, glu_b,
              ctx_len):
    wb, wc, avec = _s5_weights(lam_re, lam_im, log_step, b_re, b_im, c_re, c_im)
    y = _seg_permute(_s5_scan(_seg_permute(u, ctx_len, False), wb, wc, avec, d_skip, ctx_len), ctx_len, True)
    return _matmul(y, glu_w, layer=layer, mode="glu", bias=glu_b.astype(F32), res=xa, gate=gate,
                   ctx_len=ctx_len)


def _qk_prep_kernel(x_ref, g_ref, cos_ref, sin_ref, o_ref, *, dh, scale):
    cos = cos_ref[...]
    sin = sin_ref[...]
    g = g_ref[...]
    lane = lax.broadcasted_iota(jnp.int32, cos.shape, 1)
    quarter = dh // 4
    first = (lane // quarter) % 2 == 0
    for s in range(x_ref.shape[1] // dh):
        x = x_ref[:, s * dh:(s + 1) * dh]
        y = x * lax.rsqrt(jnp.mean(x * x, axis=-1, keepdims=True) + RMS_EPS) * g
        partner = jnp.where(first, pltpu.roll(y, dh - quarter, axis=1), pltpu.roll(y, quarter, axis=1))
        o_ref[:, s * dh:(s + 1) * dh] = ((y * cos + partner * sin) * scale).astype(o_ref.dtype)


def _qk_prep(x, gain, cos, sin, first_block, nblk, scale):
    t = x.shape[0]
    dh = gain.shape[0]
    width = 4 * dh
    tm = _pick(t, (768, 512, 256))
    return pl.pallas_call(
        functools.partial(_qk_prep_kernel, dh=dh, scale=scale),
        grid=(t // tm, nblk),
        in_specs=[
            pl.BlockSpec((tm, width), lambda i, j: (i, j + first_block)),
            pl.BlockSpec((1, dh), lambda i, j: (0, 0)),
            pl.BlockSpec((tm, dh), lambda i, j: (i, 0)),
            pl.BlockSpec((tm, dh), lambda i, j: (i, 0)),
        ],
        out_specs=pl.BlockSpec((tm, width), lambda i, j: (i, j)),
        out_shape=jax.ShapeDtypeStruct((t, nblk * width), BF16),
        compiler_params=_params("parallel", "parallel"),
    )(x, gain.astype(F32).reshape(1, dh), cos, sin)


def _flash_kernel(q_ref, k_ref, v_ref, lam_ref, g_ref, o_ref, m_ref, l_ref, acc_ref, sa_ref, sb_ref, *,
                  ctx_len, tk, dh, post_scale):
    i = pl.program_id(1)
    total = k_ref.shape[0]
    hw = 2 * dh
    nlat = (total - ctx_len) // tk
    m_ref[...] = jnp.full_like(m_ref, -jnp.inf)
    l_ref[...] = jnp.zeros_like(l_ref)
    acc_ref[...] = jnp.zeros_like(acc_ref)

    def lane_groups(x):
        return [x[:, c * LANES:(c + 1) * LANES] for c in range(x.shape[1] // LANES)]

    def scores(j, start, size):
        return lax.dot_general(q_ref[:, j * dh:(j + 1) * dh], k_ref[pl.ds(start, size), j * dh:(j + 1) * dh],
                               (((1,), (1,)), ((), ())), preferred_element_type=F32)

    def absorb(j, s, start, size):
        group_max = functools.reduce(jnp.maximum, lane_groups(s))
        m_old = m_ref[j]
        m_new = jnp.maximum(m_old, jnp.max(group_max, axis=-1, keepdims=True))
        alpha = jnp.exp2(m_old - m_new)
        p = jnp.exp2(s - jnp.tile(m_new, (1, size // LANES)))
        l_ref[j] = alpha * l_ref[j] + functools.reduce(jnp.add, lane_groups(p))
        acc_ref[j] = (jnp.tile(alpha, (1, hw // LANES)) * acc_ref[j]
                      + jnp.dot(p.astype(BF16), v_ref[pl.ds(start, size), :], preferred_element_type=F32))
        m_ref[j] = m_new

    def lat_start(c):
        return pl.multiple_of(ctx_len + c * tk, math.gcd(ctx_len, tk))

    for j in range(2):
        absorb(j, scores(j, 0, ctx_len), 0, ctx_len)

    def stage(c_next, dst_ref, c_cur, src_ref):
        if c_next is not None:
            for j in range(2):
                dst_ref[j] = scores(j, lat_start(c_next), tk)
        if c_cur is not None:
            for j in range(2):
                absorb(j, src_ref[j], lat_start(c_cur), tk)

    @pl.when(i > 0)
    def _():
        npairs = (nlat - 1) // 2
        stage(0, sa_ref, None, None)

        def body(p, carry):
            stage(2 * p + 1, sb_ref, 2 * p, sa_ref)
            stage(2 * p + 2, sa_ref, 2 * p + 1, sb_ref)
            return carry

        lax.fori_loop(0, npairs, body, 0)
        if nlat - 2 * npairs == 2:
            stage(nlat - 1, sb_ref, nlat - 2, sa_ref)
            stage(None, None, nlat - 1, sb_ref)
        else:
            stage(None, None, nlat - 1, sa_ref)

    l0 = jnp.sum(l_ref[0], axis=-1, keepdims=True)
    l1 = jnp.sum(l_ref[1], axis=-1, keepdims=True)
    o = acc_ref[0] / l0 - lam_ref[...] * (acc_ref[1] / l1)
    o = o * lax.rsqrt(jnp.mean(o * o, axis=-1, keepdims=True) + RMS_EPS) * g_ref[...] * post_scale
    o_ref[...] = o.astype(o_ref.dtype)


def _flash(q, k, v, lam, sub_gain, ctx_len, post_scale):
    t, d = q.shape
    dh = sub_gain.shape[0] // 2
    hw = 2 * dh
    tq = ctx_len
    tk = _pick(t - ctx_len, (1024, 512, 256, 128))
    lamv = jnp.full((1, hw), lam, F32)
    return pl.pallas_call(
        functools.partial(_flash_kernel, ctx_len=ctx_len, tk=tk, dh=dh, post_scale=post_scale),
        grid=(d // hw, t // tq),
        in_specs=[
            pl.BlockSpec((tq, hw), lambda h, i: (i, h)),
            pl.BlockSpec((t, hw), lambda h, i: (0, h)),
            pl.BlockSpec((t, hw), lambda h, i: (0, h)),
            pl.BlockSpec((1, hw), lambda h, i: (0, 0)),
            pl.BlockSpec((1, hw), lambda h, i: (0, 0)),
        ],
        out_specs=pl.BlockSpec((tq, hw), lambda h, i: (i, h)),
        out_shape=jax.ShapeDtypeStruct((t, d), BF16),
        scratch_shapes=[pltpu.VMEM((2, tq, LANES), F32), pltpu.VMEM((2, tq, LANES), F32),
                        pltpu.VMEM((2, tq, hw), F32), pltpu.VMEM((2, tq, tk), F32), pltpu.VMEM((2, tq, tk), F32)],
        compiler_params=_params("parallel", "arbitrary"),
    )(q, k, v, lamv, sub_gain.astype(F32).reshape(1, hw))


def _rope_tables(seq, ctx_len, dh):
    rows = seq // GRID_W
    row = jnp.repeat(jnp.arange(rows, dtype=F32), GRID_W)
    col = jnp.tile(jnp.arange(GRID_W, dtype=F32), rows)
    n_freq = dh // 4
    inv_freq = ROPE_BASE ** (-jnp.arange(n_freq, dtype=F32) / n_freq)
    ang_r = row[:, None] * inv_freq
    ang_c = col[:, None] * inv_freq
    cos = jnp.concatenate([jnp.cos(ang_r), jnp.cos(ang_r), jnp.cos(ang_c), jnp.cos(ang_c)], axis=-1)
    sin = jnp.concatenate([-jnp.sin(ang_r), jnp.sin(ang_r), -jnp.sin(ang_c), jnp.sin(ang_c)], axis=-1)
    cos = jnp.concatenate([jnp.ones((ctx_len, dh), F32), cos], axis=0)
    sin = jnp.concatenate([jnp.zeros((ctx_len, dh), F32), sin], axis=0)
    return cos, sin


def _diff_attn_mixer(h, xa, gate, layer, w_q, w_k, w_v, w_o, q_gain, k_gain, lam_q1, lam_k1, lam_q2, lam_k2,
                     sub_gain, lambda_init, ctx_len):
    t, d = h.shape
    dh = q_gain.shape[0]
    v = _matmul(h, w_v, layer=layer, out_dtype=BF16)
    cos, sin = _rope_tables(t - ctx_len, ctx_len, dh)
    nblk = d // (4 * dh)
    q = _qk_prep(_matmul(h, w_q, layer=layer, out_dtype=F32), q_gain, cos, sin, 0, nblk,
                 dh ** -0.5 * math.log2(math.e))
    k = _qk_prep(_matmul(h, w_k, layer=layer, out_dtype=F32), k_gain, cos, sin, 0, nblk, 1.0)
    lam = (jnp.exp(jnp.sum(lam_q1.astype(F32) * lam_k1.astype(F32)))
           - jnp.exp(jnp.sum(lam_q2.astype(F32) * lam_k2.astype(F32))) + lambda_init)
    o = _flash(q, k, v, lam, sub_gain, ctx_len, 1.0 - lambda_init)
    return _matmul(o, w_o, layer=layer, mode="res", res=xa, gate=gate, ctx_len=ctx_len)


def _conv_ffn(hf, xa, gate, layer, w_up, conv_w, conv_b, w_down, ctx_len):
    tgt = _matmul_conv(hf, w_up, conv_w, conv_b, layer=layer, glu=True, ctx_len=ctx_len)
    return _matmul(tgt, w_down, layer=layer, mode="res", res=xa, gate=gate, ctx_len=ctx_len)


def kernel(x, c, ctx, c_ctx, mod_w, mod_b, norm_mix, norm_ffn, ffn_up, ffn_conv_w, ffn_conv_b, ffn_down, ssd_w_in, ssd_conv_w, ssd_conv_b, ssd_a_log, ssd_dt_bias, ssd_d, ssd_norm, ssd_w_out, s5_lam_re, s5_lam_im, s5_log_step, s5_b_re, s5_b_im, s5_c_re, s5_c_im, s5_d, s5_glu_w, s5_glu_b, da_w_q, da_w_k, da_w_v, da_w_o, da_q_norm, da_k_norm, da_lam_q1, da_lam_k1, da_lam_q2, da_lam_k2, da_sub_norm):
    bsz, seq, d = x.shape
    assert bsz == 1
    ctx_len = ctx.shape[1]
    depth = mod_w.shape[0]
    cond = jnp.concatenate([jax.nn.silu(c.astype(F32)), jax.nn.silu(c_ctx.astype(F32))[None, :]], axis=0)
    mod = _modulation(cond, mod_w, mod_b).reshape(depth, 2, 6, d)
    xa = jnp.concatenate([ctx[0], x[0]], axis=0).astype(F32)

    for i in range(depth):
        kind, j = i % N_MIXERS, i // N_MIXERS
        m = mod[i]
        g1, g2 = m[:, 2], m[:, 5]
        h = _norm_mod(xa, norm_mix[i], m, 0, 1, ctx_len, F32 if kind == 1 else BF16)
        if kind == 0:
            xa = _ssd_mixer(h, xa, g1, j, ssd_w_in, ssd_conv_w[j], ssd_conv_b[j], ssd_a_log[j], ssd_dt_bias[j],
                            ssd_d[j], ssd_norm[j], ssd_w_out, ctx_len)
        elif kind == 1:
            xa = _s5_mixer(h, xa, g1, j, s5_lam_re[j], s5_lam_im[j], s5_log_step[j], s5_b_re[j], s5_b_im[j],
                           s5_c_re[j], s5_c_im[j], s5_d[j], s5_glu_w, s5_glu_b[j], ctx_len)
        else:
            lambda_init = 0.8 - 0.6 * math.exp(-0.3 * i)
            xa = _diff_attn_mixer(h, xa, g1, j, da_w_q, da_w_k, da_w_v, da_w_o, da_q_norm[j],
                                  da_k_norm[j], da_lam_q1[j], da_lam_k1[j], da_lam_q2[j], da_lam_k2[j],
                                  da_sub_norm[j], lambda_init, ctx_len)
        hf = _norm_mod(xa, norm_ffn[i], m, 3, 4, ctx_len, BF16)
        xa = _conv_ffn(hf, xa, g2, i, ffn_up, ffn_conv_w[i], ffn_conv_b[i], ffn_down, ctx_len)
    return xa[ctx_len:][None]
```

```python
import functools
import math

import jax
import jax.numpy as jnp
from jax import lax
from jax.experimental import pallas as pl
from jax.experimental.pallas import tpu as pltpu

F32 = jnp.float32
BF16 = jnp.bfloat16

RMS_EPS = 1e-6
ROPE_BASE = 10000.0
GRID_W = 64
N_MIXERS = 3

LANES = 128
SUBLANES = 8
VMEM_LIMIT_BYTES = 52 * 1024 * 1024

SSD_GROUPS = 8
SSD_HEAD_DIM = 64
SSD_STATE = 128
SSD_CHUNK = 128
S5_SEGMENTS = SUBLANES
S5_TIME_TILE = 64


def _params(*semantics):
    return pltpu.CompilerParams(dimension_semantics=semantics, vmem_limit_bytes=VMEM_LIMIT_BYTES)


def _pick(n, candidates):
    for c in candidates:
        if n % c == 0:
            return c
    raise ValueError(f"no tile in {candidates} divides {n}")


def _sigmoid(x):
    return 1.0 / (1.0 + jnp.exp(-x))


def _silu(x):
    return x * _sigmoid(x)


def _mod_kernel(c_ref, w_ref, b_ref, o_ref, acc_ref, *, nk):
    k = pl.program_id(2)

    @pl.when(k == 0)
    def _():
        acc_ref[...] = jnp.zeros_like(acc_ref)

    w = w_ref[0]
    tk, tn = w.shape
    for r in range(2):
        cb = jnp.tile(c_ref[r], (1, tn // LANES))
        acc_ref[r] += (w * cb).reshape(tk // SUBLANES, SUBLANES, tn).sum(axis=0)

    @pl.when(k == nk - 1)
    def _():
        o_ref[0] = acc_ref[...].sum(axis=1) + b_ref[0]


def _modulation(cond, mod_w, mod_b):
    depth, d, n = mod_w.shape
    tk = _pick(d, (1024, 512, 256, 128))
    tn = _pick(n, (2048, 1024, 512))
    nk = d // tk
    condb = jnp.broadcast_to(cond[:, :, None], (2, d, LANES))
    return pl.pallas_call(
        functools.partial(_mod_kernel, nk=nk),
        grid=(depth, n // tn, nk),
        in_specs=[
            pl.BlockSpec((2, tk, LANES), lambda l, j, k: (0, k, 0)),
            pl.BlockSpec((1, tk, tn), lambda l, j, k: (l, k, j)),
            pl.BlockSpec((1, 1, tn), lambda l, j, k: (l, 0, j)),
        ],
        out_specs=pl.BlockSpec((1, 2, tn), lambda l, j, k: (l, 0, j)),
        out_shape=jax.ShapeDtypeStruct((depth, 2, n), F32),
        scratch_shapes=[pltpu.VMEM((2, SUBLANES, tn), F32)],
        compiler_params=_params("parallel", "parallel", "arbitrary"),
    )(condb, mod_w, mod_b.reshape(depth, 1, n))


def _norm_mod_kernel(x_ref, g_ref, m_ref, o_ref, *, shift_row, scale_row):
    x = x_ref[...]
    y = x * lax.rsqrt(jnp.mean(x * x, axis=-1, keepdims=True) + RMS_EPS) * g_ref[...]
    m = m_ref[0]
    o_ref[...] = (y * (1.0 + m[scale_row:scale_row + 1]) + m[shift_row:shift_row + 1]).astype(o_ref.dtype)


def _norm_mod(xa, gain, mod, shift_row, scale_row, ctx_len, out_dtype):
    t, d = xa.shape
    tm = _pick(ctx_len, (256, 128))
    ncb = ctx_len // tm
    return pl.pallas_call(
        functools.partial(_norm_mod_kernel, shift_row=shift_row, scale_row=scale_row),
        grid=(t // tm,),
        in_specs=[
            pl.BlockSpec((tm, d), lambda i: (i, 0)),
            pl.BlockSpec((1, d), lambda i: (0, 0)),
            pl.BlockSpec((1, 6, d), lambda i: (jnp.where(i < ncb, 1, 0), 0, 0)),
        ],
        out_specs=pl.BlockSpec((tm, d), lambda i: (i, 0)),
        out_shape=jax.ShapeDtypeStruct((t, d), out_dtype),
        compiler_params=_params("parallel"),
    )(xa, gain.reshape(1, d), mod)


def _gate_rows(gate_ref, i, tm, tn, ctx_len):
    rows = i * tm + lax.broadcasted_iota(jnp.int32, (tm, tn), 0)
    return jnp.where(rows < ctx_len, gate_ref[1:2, :], gate_ref[0:1, :])


def _mm_kernel(*refs, nk, mode, ctx_len, cast_w):
    i = pl.program_id(1)
    k = pl.program_id(2)
    nw = 2 if mode == "glu" else 1
    a_ref, w_refs, rest = refs[0], refs[1:1 + nw], refs[1 + nw:]
    if mode == "plain":
        o_ref, scratch = rest[0], rest[1:]
    elif mode == "res":
        (res_ref, gate_ref, o_ref), scratch = rest[:3], rest[3:]
    else:
        (ba_ref, bb_ref, res_ref, gate_ref, o_ref), scratch = rest[:5], rest[5:]

    if cast_w:
        wbf, scratch = scratch[:nw], scratch[nw:]

        @pl.when(i == 0)
        def _():
            for dst, src in zip(wbf, w_refs):
                dst[...] = src[...].astype(BF16)

        w_refs = wbf
    accs = scratch

    a = a_ref[...]
    prods = [jnp.dot(a, w[...], preferred_element_type=F32) for w in w_refs]

    def finish(vals):
        tm, tn = vals[0].shape
        if mode == "plain":
            o_ref[...] = vals[0].astype(o_ref.dtype)
        elif mode == "res":
            o_ref[...] = res_ref[...] + _gate_rows(gate_ref, i, tm, tn, ctx_len) * vals[0]
        else:
            o = (vals[0] + ba_ref[...]) * _sigmoid(vals[1] + bb_ref[...])
            o_ref[...] = res_ref[...] + _gate_rows(gate_ref, i, tm, tn, ctx_len) * o

    if nk == 1:
        finish(prods)
    else:
        @pl.when(k == 0)
        def _():
            for acc, p in zip(accs, prods):
                acc[...] = p

        @pl.when(k > 0)
        def _():
            for acc, p in zip(accs, prods):
                acc[...] += p

        @pl.when(k == nk - 1)
        def _():
            finish([acc[...] for acc in accs])


def _weight_spec(w, layer, tk, tn, col_block):
    if w.ndim == 3:
        return pl.BlockSpec((None, tk, tn), lambda j, i, k: (layer, k, j + col_block))
    return pl.BlockSpec((tk, tn), lambda j, i, k: (k, j + col_block))


def _matmul(a, w, *, layer=0, n=None, col_off=0, out_dtype=F32, mode="plain", res=None, gate=None, bias=None,
            ctx_len=0):
    m, kdim = a.shape
    n_w = w.shape[-1]
    nw = 2 if mode == "glu" else 1
    if n is None:
        n = n_w // nw
    tm = _pick(m, (1056, 768, 512, 256))
    tk = _pick(kdim, (4096, 2048, 1024, 512))
    nk = kdim // tk
    cast_w = w.dtype == F32 and nk == 1
    if w.dtype == F32 and not cast_w:
        w = (w[layer] if w.ndim == 3 else w).astype(BF16)
    tn = n if n < LANES else _pick(n, (256, 128) if (cast_w and nw == 2) else (512, 256, 128))
    assert col_off % tn == 0
    nb, cb = n // tn, col_off // tn
    a_spec = pl.BlockSpec((tm, tk), lambda j, i, k: (i, k))
    o_spec = pl.BlockSpec((tm, tn), lambda j, i, k: (i, j))
    w_spec = _weight_spec(w, layer, tk, tn, cb)
    row_spec = pl.BlockSpec((1, tn), lambda j, i, k: (0, j))
    gate_spec = pl.BlockSpec((2, tn), lambda j, i, k: (0, j))
    if mode == "plain":
        in_specs, args = [a_spec, w_spec], (a, w)
    elif mode == "res":
        in_specs, args = [a_spec, w_spec, o_spec, gate_spec], (a, w, res, gate)
    else:
        wb_spec = _weight_spec(w, layer, tk, tn, nb)
        bb_spec = pl.BlockSpec((1, tn), lambda j, i, k: (0, j + nb))
        b2 = bias.reshape(1, n_w)
        in_specs = [a_spec, w_spec, wb_spec, row_spec, bb_spec, o_spec, gate_spec]
        args = (a, w, w, b2, b2, res, gate)
    scratch = [pltpu.VMEM((tk, tn), BF16)] * nw if cast_w else []
    scratch += [pltpu.VMEM((tm, tn), F32)] * nw if nk > 1 else []
    semantics = ("arbitrary",) * 3 if cast_w else ("parallel", "parallel", "arbitrary")
    return pl.pallas_call(
        functools.partial(_mm_kernel, nk=nk, mode=mode, ctx_len=ctx_len, cast_w=cast_w),
        grid=(nb, m // tm, nk),
        in_specs=in_specs,
        out_specs=o_spec,
        out_shape=jax.ShapeDtypeStruct((m, n), out_dtype),
        scratch_shapes=scratch,
        compiler_params=_params(*semantics),
    )(*args)


CONV_HALO = 16


def _mm_conv_kernel(*refs, glu, ctx_len, total, tm, cast_w):
    i = pl.program_id(1)
    nw = 2 if glu else 1
    a_ref, ap_ref, an_ref = refs[:3]
    w_refs = refs[3:3 + nw]
    cw_refs = refs[3 + nw:3 + 2 * nw]
    cb_refs = refs[3 + 2 * nw:3 + 3 * nw]
    o_ref = refs[3 + 3 * nw]
    aext_ref = refs[4 + 3 * nw]
    if cast_w:
        wbf = refs[5 + 3 * nw:5 + 4 * nw]

        @pl.when(i == 0)
        def _():
            for dst, src in zip(wbf, w_refs):
                dst[...] = src[...].astype(BF16)

        w_refs = wbf

    hb = CONV_HALO
    aext_ref[0:hb, :] = ap_ref[...]
    aext_ref[hb:hb + tm, :] = a_ref[...]
    aext_ref[hb + tm:, :] = an_ref[...]
    a = aext_ref[...]
    tn = o_ref.shape[1]
    accs = [jnp.dot(a, w_ref[...], preferred_element_type=F32) for w_ref in w_refs]

    def conv(masked):
        row = i * tm + lax.broadcasted_iota(jnp.int32, (tm, LANES), 0)
        in_ctx = row < ctx_len
        outs = []
        for acc, cw_ref, cb_ref in zip(accs, cw_refs, cb_refs):
            taps = cw_ref.shape[0]
            rad = taps // 2
            out = cb_ref[...] + acc[hb:hb + tm] * cw_ref[rad:rad + 1, :]
            for j in range(taps):
                s = j - rad
                if s == 0:
                    continue
                shifted = pltpu.roll(acc, (-s) % acc.shape[0], axis=0)[hb:hb + tm]
                if masked:
                    src = row + s
                    same_seq = jnp.logical_not(jnp.logical_xor(in_ctx, src < ctx_len))
                    ok = jnp.logical_and(jnp.logical_and(src >= 0, src < total), same_seq)
                    shifted = shifted * jnp.tile(jnp.where(ok, 1.0, 0.0), (1, tn // LANES))
                out = out + shifted * cw_ref[j:j + 1, :]
            outs.append(out)
        o = _silu(outs[0]) * outs[1] if glu else _silu(outs[0])
        o_ref[...] = o.astype(o_ref.dtype)

    start = i * tm
    near_edge = jnp.logical_or(jnp.logical_or(start == 0, start + tm >= total),
                               jnp.logical_and(start < ctx_len + hb, start + tm + hb > ctx_len))
    pl.when(near_edge)(lambda: conv(True))
    pl.when(jnp.logical_not(near_edge))(lambda: conv(False))


def _matmul_conv(a, w, conv_w, conv_b, *, layer=0, n=None, col_off=0, glu=False, ctx_len=0, out_dtype=BF16):
    m, kdim = a.shape
    nw = 2 if glu else 1
    if n is None:
        n = w.shape[-1] // nw
    tm = _pick(m, (768, 512, 256))
    cast_w = w.dtype == F32
    tn = _pick(n, (256, 128) if (cast_w and glu) else (512, 256, 128))
    assert col_off % tn == 0 and tm % CONV_HALO == 0 and m % CONV_HALO == 0
    nb, cb = n // tn, col_off // tn
    hb = CONV_HALO
    rb, nhb = tm // hb, m // hb
    taps = conv_w.shape[0]
    a_spec = pl.BlockSpec((tm, kdim), lambda j, i, k: (i, 0))
    ap_spec = pl.BlockSpec((hb, kdim), lambda j, i, k: (jnp.maximum(i * rb - 1, 0), 0))
    an_spec = pl.BlockSpec((hb, kdim), lambda j, i, k: (jnp.minimum((i + 1) * rb, nhb - 1), 0))
    cb2 = conv_b.astype(F32).reshape(1, -1)
    cw2 = conv_w.astype(F32)
    in_specs = [a_spec, ap_spec, an_spec]
    in_specs += [_weight_spec(w, layer, kdim, tn, cb + h * nb) for h in range(nw)]
    in_specs += [pl.BlockSpec((taps, tn), lambda j, i, k, h=h: (0, j + h * nb)) for h in range(nw)]
    in_specs += [pl.BlockSpec((1, tn), lambda j, i, k, h=h: (0, j + h * nb)) for h in range(nw)]
    scratch = [pltpu.VMEM((tm + 2 * hb, kdim), BF16)]
    scratch += [pltpu.VMEM((kdim, tn), BF16)] * nw if cast_w else []
    return pl.pallas_call(
        functools.partial(_mm_conv_kernel, glu=glu, ctx_len=ctx_len, total=m, tm=tm, cast_w=cast_w),
        grid=(nb, m // tm, 1),
        in_specs=in_specs,
        out_specs=pl.BlockSpec((tm, tn), lambda j, i, k: (i, j)),
        out_shape=jax.ShapeDtypeStruct((m, n), out_dtype),
        scratch_shapes=scratch,
        compiler_params=_params("arbitrary", "arbitrary", "arbitrary"),
    )(a, a, a, *([w] * nw), *([cw2] * nw), *([cb2] * nw))


def _split3(v):
    h1 = v.astype(BF16)
    r1 = v - h1.astype(F32)
    h2 = r1.astype(BF16)
    h3 = (r1 - h2.astype(F32)).astype(BF16)
    return h1, h2, h3


LOG2E = math.log2(math.e)


def _ssd_dt_kernel(raw_ref, bias_ref, a_ref, dt_ref, cum_ref, src_ref, tot_ref, *, nfwd):
    x = raw_ref[...] + bias_ref[...]
    dt = jnp.maximum(x, 0.0) + jnp.log(1.0 + jnp.exp(-jnp.abs(x)))
    dt_ref[...] = dt
    da = dt * (a_ref[...] * LOG2E)
    l = da.shape[0]
    row = lax.broadcasted_iota(jnp.int32, (l, l), 0)
    col = lax.broadcasted_iota(jnp.int32, (l, l), 1)
    incl_m = jnp.where(row >= col, 1.0, 0.0).astype(BF16)
    excl_m = jnp.where(row > col, 1.0, 0.0).astype(BF16)
    parts = _split3(da)
    incl = sum(jnp.dot(incl_m, p, preferred_element_type=F32) for p in parts)
    excl = sum(jnp.dot(excl_m, p, preferred_element_type=F32) for p in parts)
    lane = lax.broadcasted_iota(jnp.int32, da.shape, 1)
    log2dt = jnp.log(dt) * LOG2E
    cum_ref[...] = jnp.where(lane < nfwd, incl, excl)
    src_ref[...] = jnp.where(lane < nfwd, incl - log2dt, excl + log2dt)
    tot_ref[0] = incl[l - 1:l, :]


def _ssd_dt(dt_raw, dt_bias, a_neg):
    t, c = dt_raw.shape
    nc = t // SSD_CHUNK
    row = pl.BlockSpec((1, c), lambda i: (0, 0))
    blk = pl.BlockSpec((SSD_CHUNK, c), lambda i: (i, 0))
    return pl.pallas_call(
        functools.partial(_ssd_dt_kernel, nfwd=c // 2),
        grid=(nc,),
        in_specs=[blk, row, row],
        out_specs=[blk, blk, blk, pl.BlockSpec((1, 1, c), lambda i: (i, 0, 0))],
        out_shape=[jax.ShapeDtypeStruct((t, c), F32)] * 3 + [jax.ShapeDtypeStruct((nc, 1, c), F32)],
        compiler_params=_params("parallel"),
    )(dt_raw, dt_bias.reshape(1, c), a_neg.reshape(1, c))


def _ssd_scan_kernel(xf_ref, bf_ref, cf_ref, pf_ref, ptf_ref, totf_ref, tef_ref,
                     xb_ref, bb_ref, cb_ref, pb_ref, totb_ref, teb_ref,
                     e_ref, dsk_ref, yf_ref, yb_ref, h_ref, *, hpg):
    pdim = SSD_HEAD_DIM
    j = pl.program_id(1)

    @pl.when(j == 0)
    def _():
        h_ref[...] = jnp.zeros_like(h_ref)

    e = e_ref[...]

    def expand(v):
        return jnp.dot(v.astype(BF16), e, preferred_element_type=F32)

    def state_update(d, x32, bm, scale, total_decay):
        xw = (x32 * expand(scale)).astype(BF16)
        inc = lax.dot_general(bm, xw, (((0,), (0,)), ((), ())), preferred_element_type=F32)
        h_ref[d] = h_ref[d] * total_decay + inc

    x = xf_ref[...]
    x32 = x.astype(F32)
    bm = bf_ref[...]
    cm = cf_ref[...]
    p = pf_ref[0]
    pt = ptf_ref[0]
    dtf, cumf = p[:, 0:hpg], p[:, hpg:2 * hpg]
    dtb, cumb = p[:, 2 * hpg:3 * hpg], p[:, 3 * hpg:4 * hpg]
    totf = totf_ref[0, 0][:, 0:hpg]
    l = x.shape[0]
    cb = lax.dot_general(cm, bm, (((1,), (1,)), ((), ())), preferred_element_type=F32)
    li = lax.broadcasted_iota(jnp.int32, (l, l), 0)
    si = lax.broadcasted_iota(jnp.int32, (l, l), 1)
    lower = li >= si
    lane = lax.broadcasted_iota(jnp.int32, (l, 2 * pdim), 1)
    cm32 = cm.astype(F32)
    hf = h_ref[0].astype(BF16)

    def backward_output():
        pb = pb_ref[0]
        totb = totb_ref[0, 0][:, hpg:2 * hpg]
        yb_ref[...] = (jnp.dot(cb_ref[...], h_ref[1].astype(BF16), preferred_element_type=F32)
                       * expand(jnp.exp2(totb - pb[:, 3 * hpg:4 * hpg])))

    def backward_state():
        pb = pb_ref[0]
        state_update(1, xb_ref[...].astype(F32), bb_ref[...],
                     pb[:, 2 * hpg:3 * hpg] * jnp.exp2(pb[:, 3 * hpg:4 * hpg]), teb_ref[0, 0])

    def forward_state():
        state_update(0, x32, bm, dtf * jnp.exp2(totf - cumf), tef_ref[0, 0])

    npair = hpg // 2
    spread = [(npair // 4, backward_output), (npair // 2, backward_state), (3 * npair // 4, forward_state)]
    ys = []
    for q in range(npair):
        rhs = jnp.concatenate([x[:, q * 2 * pdim:(q + 1) * 2 * pdim], hf[:, q * 2 * pdim:(q + 1) * 2 * pdim]],
                              axis=0)
        rs = []
        for h in (2 * q, 2 * q + 1):
            cf_col = jnp.broadcast_to(cumf[:, h:h + 1], (l, l))
            expo = jnp.where(lower, cf_col - pt[h:h + 1, :], pt[hpg + h:hpg + h + 1, :] - cumb[:, h:h + 1])
            lhs = jnp.concatenate([(cb * jnp.exp2(expo)).astype(BF16), (cm32 * jnp.exp2(cf_col)).astype(BF16)],
                                  axis=1)
            rs.append(jnp.dot(lhs, rhs, preferred_element_type=F32))
        ys.append(jnp.where(lane < pdim, rs[0], rs[1]))
        for after, work in spread:
            if after == q:
                work()
    cb_diag = jnp.sum(cm32 * bm.astype(F32), axis=1, keepdims=True)
    yf_ref[...] = jnp.concatenate(ys, axis=1) + (dsk_ref[...] + expand(cb_diag * dtb)) * x32


def _ssd_scan(xbc, dt, cum, src, tot, d_skip, ctx_len):
    t = xbc.shape[0]
    g, n, pdim, lc = SSD_GROUPS, SSD_STATE, SSD_HEAD_DIM, SSD_CHUNK
    heads = dt.shape[1] // 2
    hpg = heads // g
    gw = hpg * pdim
    nc, ncc = t // lc, ctx_len // lc
    assert (g * gw) % n == 0
    b_off, c_off = g * gw // n, g * gw // n + g

    dt4 = dt.reshape(t, 2, g, hpg)
    cum4 = cum.reshape(t, 2, g, hpg)
    pk = jnp.concatenate([dt4[:, 0], cum4[:, 0], dt4[:, 1], cum4[:, 1]], axis=-1)
    pk = jnp.transpose(pk, (1, 0, 2))
    src4 = src.reshape(t, 2, g, hpg)
    pkt = jnp.transpose(jnp.concatenate([src4[:, 0], src4[:, 1]], axis=-1), (1, 2, 0))
    tot4 = tot.reshape(nc, 2, g, hpg)
    totg = jnp.transpose(jnp.concatenate([tot4[:, 0], tot4[:, 1]], axis=-1), (1, 0, 2))
    totg = totg.reshape(g, nc, 1, 2 * hpg)
    tote = jnp.repeat(jnp.exp2(tot4), pdim, axis=-1).reshape(nc, 2, 1, g * gw)
    e = jnp.repeat(jnp.eye(hpg, dtype=BF16), pdim, axis=1)
    dsk = jnp.repeat(d_skip.astype(F32), pdim).reshape(1, g * gw)

    def bwd_chunk(j):
        return jnp.where(j < ncc, ncc - 1 - j, nc - 1 - (j - ncc))

    def fwd_chunk(j):
        return j

    in_specs = []
    for direction, chunk in enumerate((fwd_chunk, bwd_chunk)):
        in_specs += [
            pl.BlockSpec((lc, gw), lambda gi, j, c=chunk: (c(j), gi)),
            pl.BlockSpec((lc, n), lambda gi, j, c=chunk: (c(j), b_off + gi)),
            pl.BlockSpec((lc, n), lambda gi, j, c=chunk: (c(j), c_off + gi)),
            pl.BlockSpec((1, lc, 4 * hpg), lambda gi, j, c=chunk: (gi, c(j), 0)),
        ]
        if direction == 0:
            in_specs.append(pl.BlockSpec((1, 2 * hpg, lc), lambda gi, j: (gi, 0, j)))
        in_specs += [
            pl.BlockSpec((1, 1, 1, 2 * hpg), lambda gi, j, c=chunk: (gi, c(j), 0, 0)),
            pl.BlockSpec((1, 1, 1, gw), lambda gi, j, c=chunk, dr=direction: (c(j), dr, 0, gi)),
        ]
    in_specs += [pl.BlockSpec((hpg, gw), lambda gi, j: (0, 0)),
                 pl.BlockSpec((1, gw), lambda gi, j: (0, gi))]
    out_specs = [pl.BlockSpec((lc, gw), lambda gi, j: (j, gi)),
                 pl.BlockSpec((lc, gw), lambda gi, j: (bwd_chunk(j), gi))]
    return pl.pallas_call(
        functools.partial(_ssd_scan_kernel, hpg=hpg),
        grid=(g, nc),
        in_specs=in_specs,
        out_specs=out_specs,
        out_shape=[jax.ShapeDtypeStruct((t, g * gw), F32)] * 2,
        scratch_shapes=[pltpu.VMEM((2, n, gw), F32)],
        compiler_params=_params("parallel", "arbitrary"),
    )(xbc, xbc, xbc, pk, pkt, totg, tote,
      xbc, xbc, xbc, pk, totg, tote, e, dsk)


def _ssd_finish_kernel(yf_ref, yb_ref, z_ref, w_ref, o_ref):
    y = (yf_ref[...] + yb_ref[...]) * _silu(z_ref[...].astype(F32))
    y = y * lax.rsqrt(jnp.mean(y * y, axis=-1, keepdims=True) + RMS_EPS)
    o_ref[...] = (y * w_ref[...]).astype(o_ref.dtype)


def _ssd_finish(yf, yb, z, norm_w):
    t, c = yf.shape
    gw = c // SSD_GROUPS
    tm = _pick(t, (256, 128))
    blk = pl.BlockSpec((tm, gw), lambda i, gi: (i, gi))
    return pl.pallas_call(
        _ssd_finish_kernel,
        grid=(t // tm, SSD_GROUPS),
        in_specs=[blk, blk, blk, pl.BlockSpec((1, gw), lambda i, gi: (0, gi))],
        out_specs=blk,
        out_shape=jax.ShapeDtypeStruct((t, c), BF16),
        compiler_params=_params("parallel", "parallel"),
    )(yf, yb, z, norm_w.reshape(1, c))


def _ssd_mixer(h, xa, gate, layer, w_in, conv_w, conv_b, a_log, dt_bias, d_skip, norm_w, w_out, ctx_len):
    d_inner = w_out.shape[1]
    conv_ch = conv_w.shape[1]
    n_dt = w_in.shape[2] - d_inner - conv_ch
    z = _matmul(h, w_in, layer=layer, n=d_inner, col_off=0, out_dtype=BF16)
    xbc = _matmul_conv(h, w_in, conv_w, conv_b, layer=layer, n=conv_ch, col_off=d_inner, ctx_len=ctx_len)
    dt_raw = _matmul(h, w_in, layer=layer, n=n_dt, col_off=d_inner + conv_ch, out_dtype=F32)
    a_neg = -jnp.exp(a_log.astype(F32)).reshape(-1)
    dt, cum, src, tot = _ssd_dt(dt_raw, dt_bias.astype(F32).reshape(-1), a_neg)
    yf, yb = _ssd_scan(xbc, dt, cum, src, tot, d_skip, ctx_len)
    yn = _ssd_finish(yf, yb, z, norm_w)
    return _matmul(yn, w_out, layer=layer, mode="res", res=xa, gate=gate, ctx_len=ctx_len)


def _cpow(ar, ai, n):
    rr, ri = None, None
    br, bi = ar, ai
    while n:
        if n & 1:
            if rr is None:
                rr, ri = br, bi
            else:
                rr, ri = rr * br - ri * bi, rr * bi + ri * br
        n >>= 1
        if n:
            br, bi = br * br - bi * bi, 2.0 * br * bi
    return rr, ri


def _s5_kernel(u_ref, wb_ref, wc_ref, a_ref, dsk_ref, o_ref, yacc_ref, sba_ref, sbb_ref, bua_ref, bub_ref, *,
               ctx_len, tt):
    total = u_ref.shape[0]
    half = wb_ref.shape[-1] // 2
    nseg = S5_SEGMENTS
    rowid = lax.broadcasted_iota(jnp.int32, (nseg, half), 0)
    zero = jnp.zeros((nseg, half), F32)
    yacc_ref[...] = jnp.zeros_like(yacc_ref)

    dirs = (0, 1)
    wb = [wb_ref[d, 0] for d in dirs]
    wc = [wc_ref[d, 0] for d in dirs]
    a1 = [(a_ref[d, 0][:, :half], a_ref[d, 0][:, half:]) for d in dirs]
    ab = [(jnp.broadcast_to(r, (nseg, half)), jnp.broadcast_to(i, (nseg, half))) for r, i in a1]
    hcar = [(jnp.zeros((1, half), F32), jnp.zeros((1, half), F32)) for _ in dirs]
    def step(d, bu_ref, t, sr, si):
        ar, ai = ab[d]
        br = bu_ref[d, t * nseg:(t + 1) * nseg, :half]
        bi = bu_ref[d, t * nseg:(t + 1) * nseg, half:]
        return ar * sr - ai * si + br, ar * si + ai * sr + bi

    def pipelined(ntile, project, consume, carry):
        project(0, 0)

        def body(p, c):
            project(2 * p + 1, 1)
            c = consume(2 * p, 0, c)
            project(2 * p + 2, 0)
            return consume(2 * p + 1, 1, c)

        npairs = (ntile - 1) // 2
        carry = lax.fori_loop(0, npairs, body, carry)
        if ntile - 2 * npairs == 2:
            project(ntile - 1, 1)
            carry = consume(ntile - 2, 0, carry)
            return consume(ntile - 1, 1, carry)
        return consume(ntile - 1, 0, carry)

    bu_bufs = (bua_ref, bub_ref)
    s_bufs = (sba_ref, sbb_ref)

    for base, n in ((0, ctx_len), (ctx_len, total - ctx_len)):
        ls = n // nseg
        tts = min(tt, ls)
        tile_rows = nseg * tts
        ntile = ls // tts
        steps = (list(range(tts)), list(reversed(range(tts))))

        def tile_rows_of(d, k, ls=ls, base=base, tts=tts, tile_rows=tile_rows):
            t0 = k * tts if d == 0 else ls - (k + 1) * tts
            start = base + t0 * nseg
            if not isinstance(start, int):
                start = pl.multiple_of(start, math.gcd(base, tile_rows))
            return pl.ds(start, tile_rows)

        def project(k, par, tile_rows_of=tile_rows_of, tile_rows=tile_rows):
            for d in dirs:
                bu_bufs[par][d, 0:tile_rows, :] = jnp.dot(u_ref[tile_rows_of(d, k), :].astype(BF16), wb[d],
                                                          preferred_element_type=F32)

        def pass1(k, par, carry, tts=tts, steps=steps):
            s = list(carry)
            for q in range(tts):
                for d in dirs:
                    s[2 * d], s[2 * d + 1] = step(d, bu_bufs[par], steps[d][q], s[2 * d], s[2 * d + 1])
            return tuple(s)

        sf = pipelined(ntile, project, pass1, (zero,) * 4)

        hin = []
        for d in dirs:
            alr, ali = _cpow(a1[d][0], a1[d][1], ls)
            hr, hi = hcar[d]
            hinr, hini = zero, zero
            for sg in (range(nseg) if d == 0 else reversed(range(nseg))):
                hinr = jnp.where(rowid == sg, hr, hinr)
                hini = jnp.where(rowid == sg, hi, hini)
                hr, hi = (alr * hr - ali * hi + sf[2 * d][sg:sg + 1], alr * hi + ali * hr + sf[2 * d + 1][sg:sg + 1])
            hcar[d] = (hr, hi)
            hin += [hinr, hini]

        def project_out(k, par, tile_rows_of=tile_rows_of, tile_rows=tile_rows):
            for d in dirs:
                y = jnp.dot(s_bufs[par][d, 0:tile_rows, :].astype(BF16), wc[d], preferred_element_type=F32)
                yacc_ref[tile_rows_of(d, k), :] += y

        def pass2(k, par, carry, tts=tts, steps=steps, project_out=project_out):
            project_out(k - 1 if isinstance(k, int) and k > 0 else jnp.maximum(k - 1, 0), 1 - par)
            s = list(carry)
            for q in range(tts):
                for d in dirs:
                    t = steps[d][q]
                    s[2 * d], s[2 * d + 1] = step(d, bu_bufs[par], t, s[2 * d], s[2 * d + 1])
                    s_bufs[par][d, t * nseg:(t + 1) * nseg, :half] = s[2 * d]
                    s_bufs[par][d, t * nseg:(t + 1) * nseg, half:] = s[2 * d + 1]
            return tuple(s)

        s_bufs[1][...] = jnp.zeros_like(s_bufs[1])
        pipelined(ntile, project, pass2, tuple(hin))
        project_out(ntile - 1, (ntile - 1) % 2)

    y = yacc_ref[...] + dsk_ref[...] * u_ref[...]
    o_ref[...] = jax.nn.gelu(y).astype(o_ref.dtype)


def _s5_weights(lam_re, lam_im, log_step, b_re, b_im, c_re, c_im):
    step = jnp.exp(log_step.astype(F32))[..., None]
    lr, li = lam_re.astype(F32), lam_im.astype(F32)
    mag = jnp.exp(lr * step)
    ar, ai = mag * jnp.cos(li * step), mag * jnp.sin(li * step)
    den = lr * lr + li * li
    kr = ((ar - 1.0) * lr + ai * li) / den
    ki = (ai * lr - (ar - 1.0) * li) / den
    br, bi = b_re.astype(F32), b_im.astype(F32)
    bbr = kr[..., None] * br - ki[..., None] * bi
    bbi = kr[..., None] * bi + ki[..., None] * br
    cr, ci = c_re.astype(F32), c_im.astype(F32)
    ng, pst, gc = bbr.shape[1:]
    gpb = LANES // gc
    nb = ng // gpb
    eye = jnp.eye(gpb, dtype=F32)

    def in_w(bb):
        bb = bb.reshape(2, nb, gpb, pst, gc)
        return jnp.einsum('dbgpc,gh->dbgchp', bb, eye).reshape(2, nb, gpb * gc, gpb * pst)

    def out_w(cc):
        cc = cc.reshape(2, nb, gpb, gc, pst)
        return jnp.einsum('dbgcp,gh->dbhpgc', cc, eye).reshape(2, nb, gpb * pst, gpb * gc)

    wb = jnp.concatenate([in_w(bbr), in_w(bbi)], axis=-1).astype(BF16)
    wc = jnp.concatenate([out_w(cr), out_w(-ci)], axis=-2).astype(BF16)
    avec = jnp.concatenate([ar.reshape(2, nb, 1, gpb * pst), ai.reshape(2, nb, 1, gpb * pst)], axis=-1)
    return wb, wc, avec


def _seg_permute(x, ctx_len, inverse):
    def perm(v):
        n = v.shape[0]
        shape = (n // S5_SEGMENTS, S5_SEGMENTS) if inverse else (S5_SEGMENTS, n // S5_SEGMENTS)
        return jnp.swapaxes(v.reshape(*shape, -1), 0, 1).reshape(n, -1)
    return jnp.concatenate([perm(x[:ctx_len]), perm(x[ctx_len:])], axis=0)


def _s5_scan(u, wb, wc, avec, d_skip, ctx_len):
    t, d = u.shape
    nb = d // LANES
    feat = wb.shape[-1]
    tt = S5_TIME_TILE
    for n in (ctx_len, t - ctx_len):
        assert n % S5_SEGMENTS == 0 and (n // S5_SEGMENTS) % min(tt, n // S5_SEGMENTS) == 0
    return pl.pallas_call(
        functools.partial(_s5_kernel, ctx_len=ctx_len, tt=tt),
        grid=(nb,),
        in_specs=[
            pl.BlockSpec((t, LANES), lambda b: (0, b)),
            pl.BlockSpec((2, 1, LANES, feat), lambda b: (0, b, 0, 0)),
            pl.BlockSpec((2, 1, feat, LANES), lambda b: (0, b, 0, 0)),
            pl.BlockSpec((2, 1, 1, feat), lambda b: (0, b, 0, 0)),
            pl.BlockSpec((1, LANES), lambda b: (0, b)),
        ],
        out_specs=pl.BlockSpec((t, LANES), lambda b: (0, b)),
        out_shape=jax.ShapeDtypeStruct((t, d), BF16),
        scratch_shapes=[pltpu.VMEM((t, LANES), F32)] + [pltpu.VMEM((2, S5_SEGMENTS * tt, feat), F32)] * 4,
        compiler_params=_params("parallel"),
    )(u, wb, wc, avec, d_skip.astype(F32).reshape(1, d))


def _s5_mixer(u, xa, gate, layer, lam_re, lam_im, log_step, b_re, b_im, c_re, c_im, d_skip, glu_w, glu_b,
              ctx_len):
    wb, wc, avec = _s5_weights(lam_re, lam_im, log_step, b_re, b_im, c_re, c_im)
    y = _seg_permute(_s5_scan(_seg_permute(u, ctx_len, False), wb, wc, avec, d_skip, ctx_len), ctx_len, True)
    return _matmul(y, glu_w, layer=layer, mode="glu", bias=glu_b.astype(F32), res=xa, gate=gate,
                   ctx_len=ctx_len)


def _qk_prep_kernel(x_ref, g_ref, cos_ref, sin_ref, o_ref, *, dh, scale):
    cos = cos_ref[...]
    sin = sin_ref[...]
    g = g_ref[...]
    lane = lax.broadcasted_iota(jnp.int32, cos.shape, 1)
    quarter = dh // 4
    first = (lane // quarter) % 2 == 0
    for s in range(x_ref.shape[1] // dh):
        x = x_ref[:, s * dh:(s + 1) * dh]
        y = x * lax.rsqrt(jnp.mean(x * x, axis=-1, keepdims=True) + RMS_EPS) * g
        partner = jnp.where(first, pltpu.roll(y, dh - quarter, axis=1), pltpu.roll(y, quarter, axis=1))
        o_ref[:, s * dh:(s + 1) * dh] = ((y * cos + partner * sin) * scale).astype(o_ref.dtype)


def _qk_prep(x, gain, cos, sin, first_block, nblk, scale):
    t = x.shape[0]
    dh = gain.shape[0]
    width = 4 * dh
    tm = _pick(t, (768, 512, 256))
    return pl.pallas_call(
        functools.partial(_qk_prep_kernel, dh=dh, scale=scale),
        grid=(t // tm, nblk),
        in_specs=[
            pl.BlockSpec((tm, width), lambda i, j: (i, j + first_block)),
            pl.BlockSpec((1, dh), lambda i, j: (0, 0)),
            pl.BlockSpec((tm, dh), lambda i, j: (i, 0)),
            pl.BlockSpec((tm, dh), lambda i, j: (i, 0)),
        ],
        out_specs=pl.BlockSpec((tm, width), lambda i, j: (i, j)),
        out_shape=jax.ShapeDtypeStruct((t, nblk * width), BF16),
        compiler_params=_params("parallel", "parallel"),
    )(x, gain.astype(F32).reshape(1, dh), cos, sin)


def _flash_kernel(q_ref, k_ref, v_ref, lam_ref, g_ref, o_ref, m_ref, l_ref, acc_ref, sa_ref, sb_ref, *,
                  ctx_len, tk, dh, post_scale):
    i = pl.program_id(1)
    total = k_ref.shape[0]
    hw = 2 * dh
    nlat = (total - ctx_len) // tk
    m_ref[...] = jnp.full_like(m_ref, -jnp.inf)
    l_ref[...] = jnp.zeros_like(l_ref)
    acc_ref[...] = jnp.zeros_like(acc_ref)

    def lane_groups(x):
        return [x[:, c * LANES:(c + 1) * LANES] for c in range(x.shape[1] // LANES)]

    def scores(j, start, size):
        return lax.dot_general(q_ref[:, j * dh:(j + 1) * dh], k_ref[pl.ds(start, size), j * dh:(j + 1) * dh],
                               (((1,), (1,)), ((), ())), preferred_element_type=F32)

    def absorb(j, s, start, size):
        group_max = functools.reduce(jnp.maximum, lane_groups(s))
        m_old = m_ref[j]
        m_new = jnp.maximum(m_old, jnp.max(group_max, axis=-1, keepdims=True))
        alpha = jnp.exp2(m_old - m_new)
        p = jnp.exp2(s - jnp.tile(m_new, (1, size // LANES)))
        l_ref[j] = alpha * l_ref[j] + functools.reduce(jnp.add, lane_groups(p))
        acc_ref[j] = (jnp.tile(alpha, (1, hw // LANES)) * acc_ref[j]
                      + jnp.dot(p.astype(BF16), v_ref[pl.ds(start, size), :], preferred_element_type=F32))
        m_ref[j] = m_new

    def lat_start(c):
        return pl.multiple_of(ctx_len + c * tk, math.gcd(ctx_len, tk))

    for j in range(2):
        absorb(j, scores(j, 0, ctx_len), 0, ctx_len)

    def stage(c_next, dst_ref, c_cur, src_ref):
        if c_next is not None:
            for j in range(2):
                dst_ref[j] = scores(j, lat_start(c_next), tk)
        if c_cur is not None:
            for j in range(2):
                absorb(j, src_ref[j], lat_start(c_cur), tk)

    @pl.when(i > 0)
    def _():
        npairs = (nlat - 1) // 2
        stage(0, sa_ref, None, None)

        def body(p, carry):
            stage(2 * p + 1, sb_ref, 2 * p, sa_ref)
            stage(2 * p + 2, sa_ref, 2 * p + 1, sb_ref)
            return carry

        lax.fori_loop(0, npairs, body, 0)
        if nlat - 2 * npairs == 2:
            stage(nlat - 1, sb_ref, nlat - 2, sa_ref)
            stage(None, None, nlat - 1, sb_ref)
        else:
            stage(None, None, nlat - 1, sa_ref)

    l0 = jnp.sum(l_ref[0], axis=-1, keepdims=True)
    l1 = jnp.sum(l_ref[1], axis=-1, keepdims=True)
    o = acc_ref[0] / l0 - lam_ref[...] * (acc_ref[1] / l1)
    o = o * lax.rsqrt(jnp.mean(o * o, axis=-1, keepdims=True) + RMS_EPS) * g_ref[...] * post_scale
    o_ref[...] = o.astype(o_ref.dtype)


def _flash(q, k, v, lam, sub_gain, ctx_len, post_scale):
    t, d = q.shape
    dh = sub_gain.shape[0] // 2
    hw = 2 * dh
    tq = ctx_len
    tk = _pick(t - ctx_len, (1024, 512, 256, 128))
    lamv = jnp.full((1, hw), lam, F32)
    return pl.pallas_call(
        functools.partial(_flash_kernel, ctx_len=ctx_len, tk=tk, dh=dh, post_scale=post_scale),
        grid=(d // hw, t // tq),
        in_specs=[
            pl.BlockSpec((tq, hw), lambda h, i: (i, h)),
            pl.BlockSpec((t, hw), lambda h, i: (0, h)),
            pl.BlockSpec((t, hw), lambda h, i: (0, h)),
            pl.BlockSpec((1, hw), lambda h, i: (0, 0)),
            pl.BlockSpec((1, hw), lambda h, i: (0, 0)),
        ],
        out_specs=pl.BlockSpec((tq, hw), lambda h, i: (i, h)),
        out_shape=jax.ShapeDtypeStruct((t, d), BF16),
        scratch_shapes=[pltpu.VMEM((2, tq, LANES), F32), pltpu.VMEM((2, tq, LANES), F32),
                        pltpu.VMEM((2, tq, hw), F32), pltpu.VMEM((2, tq, tk), F32), pltpu.VMEM((2, tq, tk), F32)],
        compiler_params=_params("parallel", "arbitrary"),
    )(q, k, v, lamv, sub_gain.astype(F32).reshape(1, hw))


def _rope_tables(seq, ctx_len, dh):
    rows = seq // GRID_W
    row = jnp.repeat(jnp.arange(rows, dtype=F32), GRID_W)
    col = jnp.tile(jnp.arange(GRID_W, dtype=F32), rows)
    n_freq = dh // 4
    inv_freq = ROPE_BASE ** (-jnp.arange(n_freq, dtype=F32) / n_freq)
    ang_r = row[:, None] * inv_freq
    ang_c = col[:, None] * inv_freq
    cos = jnp.concatenate([jnp.cos(ang_r), jnp.cos(ang_r), jnp.cos(ang_c), jnp.cos(ang_c)], axis=-1)
    sin = jnp.concatenate([-jnp.sin(ang_r), jnp.sin(ang_r), -jnp.sin(ang_c), jnp.sin(ang_c)], axis=-1)
    cos = jnp.concatenate([jnp.ones((ctx_len, dh), F32), cos], axis=0)
    sin = jnp.concatenate([jnp.zeros((ctx_len, dh), F32), sin], axis=0)
    return cos, sin


def _diff_attn_mixer(h, xa, gate, layer, w_q, w_k, w_v, w_o, q_gain, k_gain, lam_q1, lam_k1, lam_q2, lam_k2,
                     sub_gain, lambda_init, ctx_len):
    t, d = h.shape
    dh = q_gain.shape[0]
    v = _matmul(h, w_v, layer=layer, out_dtype=BF16)
    cos, sin = _rope_tables(t - ctx_len, ctx_len, dh)
    nblk = d // (4 * dh)
    q = _qk_prep(_matmul(h, w_q, layer=layer, out_dtype=F32), q_gain, cos, sin, 0, nblk,
                 dh ** -0.5 * math.log2(math.e))
    k = _qk_prep(_matmul(h, w_k, layer=layer, out_dtype=F32), k_gain, cos, sin, 0, nblk, 1.0)
    lam = (jnp.exp(jnp.sum(lam_q1.astype(F32) * lam_k1.astype(F32)))
           - jnp.exp(jnp.sum(lam_q2.astype(F32) * lam_k2.astype(F32))) + lambda_init)
    o = _flash(q, k, v, lam, sub_gain, ctx_len, 1.0 - lambda_init)
    return _matmul(o, w_o, layer=layer, mode="res", res=xa, gate=gate, ctx_len=ctx_len)


def _conv_ffn(hf, xa, gate, layer, w_up, conv_w, conv_b, w_down, ctx_len):
    tgt = _matmul_conv(hf, w_up, conv_w, conv_b, layer=layer, glu=True, ctx_len=ctx_len)
    return _matmul(tgt, w_down, layer=layer, mode="res", res=xa, gate=gate, ctx_len=ctx_len)


def kernel(x, c, ctx, c_ctx, mod_w, mod_b, norm_mix, norm_ffn, ffn_up, ffn_conv_w, ffn_conv_b, ffn_down, ssd_w_in, ssd_conv_w, ssd_conv_b, ssd_a_log, ssd_dt_bias, ssd_d, ssd_norm, ssd_w_out, s5_lam_re, s5_lam_im, s5_log_step, s5_b_re, s5_b_im, s5_c_re, s5_c_im, s5_d, s5_glu_w, s5_glu_b, da_w_q, da_w_k, da_w_v, da_w_o, da_q_norm, da_k_norm, da_lam_q1, da_lam_k1, da_lam_q2, da_lam_k2, da_sub_norm):
    bsz, seq, d = x.shape
    assert bsz == 1
    ctx_len = ctx.shape[1]
    depth = mod_w.shape[0]
    cond = jnp.concatenate([jax.nn.silu(c.astype(F32)), jax.nn.silu(c_ctx.astype(F32))[None, :]], axis=0)
    mod = _modulation(cond, mod_w, mod_b).reshape(depth, 2, 6, d)
    xa = jnp.concatenate([ctx[0], x[0]], axis=0).astype(F32)

    for i in range(depth):
        kind, j = i % N_MIXERS, i // N_MIXERS
        m = mod[i]
        g1, g2 = m[:, 2], m[:, 5]
        h = _norm_mod(xa, norm_mix[i], m, 0, 1, ctx_len, F32 if kind == 1 else BF16)
        if kind == 0:
            xa = _ssd_mixer(h, xa, g1, j, ssd_w_in, ssd_conv_w[j], ssd_conv_b[j], ssd_a_log[j], ssd_dt_bias[j],
                            ssd_d[j], ssd_norm[j], ssd_w_out, ctx_len)
        elif kind == 1:
            xa = _s5_mixer(h, xa, g1, j, s5_lam_re[j], s5_lam_im[j], s5_log_step[j], s5_b_re[j], s5_b_im[j],
                           s5_c_re[j], s5_c_im[j], s5_d[j], s5_glu_w, s5_glu_b[j], ctx_len)
        else:
            lambda_init = 0.8 - 0.6 * math.exp(-0.3 * i)
            xa = _diff_attn_mixer(h, xa, g1, j, da_w_q, da_w_k, da_w_v, da_w_o, da_q_norm[j],
                                  da_k_norm[j], da_lam_q1[j], da_lam_k1[j], da_lam_q2[j], da_lam_k2[j],
                                  da_sub_norm[j], lambda_init, ctx_len)
        hf = _norm_mod(xa, norm_ffn[i], m, 3, 4, ctx_len, BF16)
        xa = _conv_ffn(hf, xa, g2, i, ffn_up, ffn_conv_w[i], ffn_conv_b[i], ffn_down, ctx_len)
    return xa[ctx_len:][None]
```

```python
import functools
import math

import jax
import jax.numpy as jnp
from jax import lax
from jax.experimental import pallas as pl
from jax.experimental.pallas import tpu as pltpu

F32 = jnp.float32
BF16 = jnp.bfloat16

RMS_EPS = 1e-6
ROPE_BASE = 10000.0
GRID_W = 64
N_MIXERS = 3

LANES = 128
SUBLANES = 8
VMEM_LIMIT_BYTES = 52 * 1024 * 1024

SSD_GROUPS = 8
SSD_HEAD_DIM = 64
SSD_STATE = 128
SSD_CHUNK = 128
S5_SEGMENTS = SUBLANES
S5_TIME_TILE = 64
FLASH_Q_ROWS = 512


def _params(*semantics):
    return pltpu.CompilerParams(dimension_semantics=semantics, vmem_limit_bytes=VMEM_LIMIT_BYTES)


def _pick(n, candidates):
    for c in candidates:
        if n % c == 0:
            return c
    raise ValueError(f"no tile in {candidates} divides {n}")


def _sigmoid(x):
    return 1.0 / (1.0 + jnp.exp(-x))


def _silu(x):
    return x * _sigmoid(x)


def _mod_kernel(c_ref, w_ref, b_ref, o_ref, acc_ref, *, nk):
    k = pl.program_id(2)

    @pl.when(k == 0)
    def _():
        acc_ref[...] = jnp.zeros_like(acc_ref)

    w = w_ref[0]
    tk, tn = w.shape
    for r in range(2):
        cb = jnp.tile(c_ref[r], (1, tn // LANES))
        acc_ref[r] += (w * cb).reshape(tk // SUBLANES, SUBLANES, tn).sum(axis=0)

    @pl.when(k == nk - 1)
    def _():
        o_ref[0] = acc_ref[...].sum(axis=1) + b_ref[0]


def _modulation(cond, mod_w, mod_b):
    depth, d, n = mod_w.shape
    tk = _pick(d, (1024, 512, 256, 128))
    tn = _pick(n, (2048, 1024, 512))
    nk = d // tk
    condb = jnp.broadcast_to(cond[:, :, None], (2, d, LANES))
    return pl.pallas_call(
        functools.partial(_mod_kernel, nk=nk),
        grid=(depth, n // tn, nk),
        in_specs=[
            pl.BlockSpec((2, tk, LANES), lambda l, j, k: (0, k, 0)),
            pl.BlockSpec((1, tk, tn), lambda l, j, k: (l, k, j)),
            pl.BlockSpec((1, 1, tn), lambda l, j, k: (l, 0, j)),
        ],
        out_specs=pl.BlockSpec((1, 2, tn), lambda l, j, k: (l, 0, j)),
        out_shape=jax.ShapeDtypeStruct((depth, 2, n), F32),
        scratch_shapes=[pltpu.VMEM((2, SUBLANES, tn), F32)],
        compiler_params=_params("parallel", "parallel", "arbitrary"),
    )(condb, mod_w, mod_b.reshape(depth, 1, n))


def _norm_mod_kernel(x_ref, g_ref, m_ref, o_ref, *, shift_row, scale_row):
    x = x_ref[...]
    y = x * lax.rsqrt(jnp.mean(x * x, axis=-1, keepdims=True) + RMS_EPS) * g_ref[...]
    m = m_ref[0]
    o_ref[...] = (y * (1.0 + m[scale_row:scale_row + 1]) + m[shift_row:shift_row + 1]).astype(o_ref.dtype)


def _norm_mod(xa, gain, mod, shift_row, scale_row, ctx_len, out_dtype):
    t, d = xa.shape
    tm = _pick(ctx_len, (256, 128))
    ncb = ctx_len // tm
    return pl.pallas_call(
        functools.partial(_norm_mod_kernel, shift_row=shift_row, scale_row=scale_row),
        grid=(t // tm,),
        in_specs=[
            pl.BlockSpec((tm, d), lambda i: (i, 0)),
            pl.BlockSpec((1, d), lambda i: (0, 0)),
            pl.BlockSpec((1, 6, d), lambda i: (jnp.where(i < ncb, 1, 0), 0, 0)),
        ],
        out_specs=pl.BlockSpec((tm, d), lambda i: (i, 0)),
        out_shape=jax.ShapeDtypeStruct((t, d), out_dtype),
        compiler_params=_params("parallel"),
    )(xa, gain.reshape(1, d), mod)


def _gate_rows(gate_ref, i, tm, tn, ctx_len):
    rows = i * tm + lax.broadcasted_iota(jnp.int32, (tm, tn), 0)
    return jnp.where(rows < ctx_len, gate_ref[1:2, :], gate_ref[0:1, :])


def _mm_kernel(*refs, nk, mode, ctx_len, cast_w):
    i = pl.program_id(1)
    k = pl.program_id(2)
    nw = 2 if mode == "glu" else 1
    a_ref, w_refs, rest = refs[0], refs[1:1 + nw], refs[1 + nw:]
    if mode == "plain":
        o_ref, scratch = rest[0], rest[1:]
    elif mode == "res":
        (res_ref, gate_ref, o_ref), scratch = rest[:3], rest[3:]
    else:
        (ba_ref, bb_ref, res_ref, gate_ref, o_ref), scratch = rest[:5], rest[5:]

    if cast_w:
        wbf, scratch = scratch[:nw], scratch[nw:]

        @pl.when(i == 0)
        def _():
            for dst, src in zip(wbf, w_refs):
                dst[...] = src[...].astype(BF16)

        w_refs = wbf
    accs = scratch

    a = a_ref[...]
    prods = [jnp.dot(a, w[...], preferred_element_type=F32) for w in w_refs]

    def finish(vals):
        tm, tn = vals[0].shape
        if mode == "plain":
            o_ref[...] = vals[0].astype(o_ref.dtype)
        elif mode == "res":
            o_ref[...] = res_ref[...] + _gate_rows(gate_ref, i, tm, tn, ctx_len) * vals[0]
        else:
            o = (vals[0] + ba_ref[...]) * _sigmoid(vals[1] + bb_ref[...])
            o_ref[...] = res_ref[...] + _gate_rows(gate_ref, i, tm, tn, ctx_len) * o

    if nk == 1:
        finish(prods)
    else:
        @pl.when(k == 0)
        def _():
            for acc, p in zip(accs, prods):
                acc[...] = p

        @pl.when(k > 0)
        def _():
            for acc, p in zip(accs, prods):
                acc[...] += p

        @pl.when(k == nk - 1)
        def _():
            finish([acc[...] for acc in accs])


def _weight_spec(w, layer, tk, tn, col_block):
    if w.ndim == 3:
        return pl.BlockSpec((None, tk, tn), lambda j, i, k: (layer, k, j + col_block))
    return pl.BlockSpec((tk, tn), lambda j, i, k: (k, j + col_block))


def _matmul(a, w, *, layer=0, n=None, col_off=0, out_dtype=F32, mode="plain", res=None, gate=None, bias=None,
            ctx_len=0):
    m, kdim = a.shape
    n_w = w.shape[-1]
    nw = 2 if mode == "glu" else 1
    if n is None:
        n = n_w // nw
    tm = _pick(m, (1056, 768, 512, 256))
    tk = _pick(kdim, (4096, 2048, 1024, 512))
    nk = kdim // tk
    cast_w = w.dtype == F32 and nk == 1
    if w.dtype == F32 and not cast_w:
        w = (w[layer] if w.ndim == 3 else w).astype(BF16)
    tn = n if n < LANES else _pick(n, (256, 128) if (cast_w and nw == 2) else (512, 256, 128))
    assert col_off % tn == 0
    nb, cb = n // tn, col_off // tn
    a_spec = pl.BlockSpec((tm, tk), lambda j, i, k: (i, k))
    o_spec = pl.BlockSpec((tm, tn), lambda j, i, k: (i, j))
    w_spec = _weight_spec(w, layer, tk, tn, cb)
    row_spec = pl.BlockSpec((1, tn), lambda j, i, k: (0, j))
    gate_spec = pl.BlockSpec((2, tn), lambda j, i, k: (0, j))
    if mode == "plain":
        in_specs, args = [a_spec, w_spec], (a, w)
    elif mode == "res":
        in_specs, args = [a_spec, w_spec, o_spec, gate_spec], (a, w, res, gate)
    else:
        wb_spec = _weight_spec(w, layer, tk, tn, nb)
        bb_spec = pl.BlockSpec((1, tn), lambda j, i, k: (0, j + nb))
        b2 = bias.reshape(1, n_w)
        in_specs = [a_spec, w_spec, wb_spec, row_spec, bb_spec, o_spec, gate_spec]
        args = (a, w, w, b2, b2, res, gate)
    scratch = [pltpu.VMEM((tk, tn), BF16)] * nw if cast_w else []
    scratch += [pltpu.VMEM((tm, tn), F32)] * nw if nk > 1 else []
    semantics = ("arbitrary",) * 3 if cast_w else ("parallel", "parallel", "arbitrary")
    return pl.pallas_call(
        functools.partial(_mm_kernel, nk=nk, mode=mode, ctx_len=ctx_len, cast_w=cast_w),
        grid=(nb, m // tm, nk),
        in_specs=in_specs,
        out_specs=o_spec,
        out_shape=jax.ShapeDtypeStruct((m, n), out_dtype),
        scratch_shapes=scratch,
        compiler_params=_params(*semantics),
    )(*args)


CONV_HALO = 16


def _mm_conv_kernel(*refs, glu, ctx_len, total, tm, cast_w):
    i = pl.program_id(1)
    nw = 2 if glu else 1
    a_ref, ap_ref, an_ref = refs[:3]
    w_refs = refs[3:3 + nw]
    cw_refs = refs[3 + nw:3 + 2 * nw]
    cb_refs = refs[3 + 2 * nw:3 + 3 * nw]
    o_ref = refs[3 + 3 * nw]
    aext_ref = refs[4 + 3 * nw]
    if cast_w:
        wbf = refs[5 + 3 * nw:5 + 4 * nw]

        @pl.when(i == 0)
        def _():
            for dst, src in zip(wbf, w_refs):
                dst[...] = src[...].astype(BF16)

        w_refs = wbf

    hb = CONV_HALO
    aext_ref[0:hb, :] = ap_ref[...]
    aext_ref[hb:hb + tm, :] = a_ref[...]
    aext_ref[hb + tm:, :] = an_ref[...]
    a = aext_ref[...]
    tn = o_ref.shape[1]
    accs = [jnp.dot(a, w_ref[...], preferred_element_type=F32) for w_ref in w_refs]

    def conv(masked):
        row = i * tm + lax.broadcasted_iota(jnp.int32, (tm, LANES), 0)
        in_ctx = row < ctx_len
        outs = []
        for acc, cw_ref, cb_ref in zip(accs, cw_refs, cb_refs):
            taps = cw_ref.shape[0]
            rad = taps // 2
            out = cb_ref[...] + acc[hb:hb + tm] * cw_ref[rad:rad + 1, :]
            for j in range(taps):
                s = j - rad
                if s == 0:
                    continue
                shifted = pltpu.roll(acc, (-s) % acc.shape[0], axis=0)[hb:hb + tm]
                if masked:
                    src = row + s
                    same_seq = jnp.logical_not(jnp.logical_xor(in_ctx, src < ctx_len))
                    ok = jnp.logical_and(jnp.logical_and(src >= 0, src < total), same_seq)
                    shifted = shifted * jnp.tile(jnp.where(ok, 1.0, 0.0), (1, tn // LANES))
                out = out + shifted * cw_ref[j:j + 1, :]
            outs.append(out)
        o = _silu(outs[0]) * outs[1] if glu else _silu(outs[0])
        o_ref[...] = o.astype(o_ref.dtype)

    start = i * tm
    near_edge = jnp.logical_or(jnp.logical_or(start == 0, start + tm >= total),
                               jnp.logical_and(start < ctx_len + hb, start + tm + hb > ctx_len))
    pl.when(near_edge)(lambda: conv(True))
    pl.when(jnp.logical_not(near_edge))(lambda: conv(False))


def _matmul_conv(a, w, conv_w, conv_b, *, layer=0, n=None, col_off=0, glu=False, ctx_len=0, out_dtype=BF16):
    m, kdim = a.shape
    nw = 2 if glu else 1
    if n is None:
        n = w.shape[-1] // nw
    tm = _pick(m, (768, 512, 256))
    cast_w = w.dtype == F32
    tn = _pick(n, (256, 128) if (cast_w and glu) else (512, 256, 128))
    assert col_off % tn == 0 and tm % CONV_HALO == 0 and m % CONV_HALO == 0
    nb, cb = n // tn, col_off // tn
    hb = CONV_HALO
    rb, nhb = tm // hb, m // hb
    taps = conv_w.shape[0]
    a_spec = pl.BlockSpec((tm, kdim), lambda j, i, k: (i, 0))
    ap_spec = pl.BlockSpec((hb, kdim), lambda j, i, k: (jnp.maximum(i * rb - 1, 0), 0))
    an_spec = pl.BlockSpec((hb, kdim), lambda j, i, k: (jnp.minimum((i + 1) * rb, nhb - 1), 0))
    cb2 = conv_b.astype(F32).reshape(1, -1)
    cw2 = conv_w.astype(F32)
    in_specs = [a_spec, ap_spec, an_spec]
    in_specs += [_weight_spec(w, layer, kdim, tn, cb + h * nb) for h in range(nw)]
    in_specs += [pl.BlockSpec((taps, tn), lambda j, i, k, h=h: (0, j + h * nb)) for h in range(nw)]
    in_specs += [pl.BlockSpec((1, tn), lambda j, i, k, h=h: (0, j + h * nb)) for h in range(nw)]
    scratch = [pltpu.VMEM((tm + 2 * hb, kdim), BF16)]
    scratch += [pltpu.VMEM((kdim, tn), BF16)] * nw if cast_w else []
    return pl.pallas_call(
        functools.partial(_mm_conv_kernel, glu=glu, ctx_len=ctx_len, total=m, tm=tm, cast_w=cast_w),
        grid=(nb, m // tm, 1),
        in_specs=in_specs,
        out_specs=pl.BlockSpec((tm, tn), lambda j, i, k: (i, j)),
        out_shape=jax.ShapeDtypeStruct((m, n), out_dtype),
        scratch_shapes=scratch,
        compiler_params=_params("arbitrary", "arbitrary", "arbitrary"),
    )(a, a, a, *([w] * nw), *([cw2] * nw), *([cb2] * nw))


def _split3(v):
    h1 = v.astype(BF16)
    r1 = v - h1.astype(F32)
    h2 = r1.astype(BF16)
    h3 = (r1 - h2.astype(F32)).astype(BF16)
    return h1, h2, h3


LOG2E = math.log2(math.e)


def _ssd_dt_kernel(raw_ref, bias_ref, a_ref, dt_ref, cum_ref, src_ref, tot_ref, *, nfwd):
    x = raw_ref[...] + bias_ref[...]
    dt = jnp.maximum(x, 0.0) + jnp.log(1.0 + jnp.exp(-jnp.abs(x)))
    dt_ref[...] = dt
    da = dt * (a_ref[...] * LOG2E)
    l = da.shape[0]
    row = lax.broadcasted_iota(jnp.int32, (l, l), 0)
    col = lax.broadcasted_iota(jnp.int32, (l, l), 1)
    incl_m = jnp.where(row >= col, 1.0, 0.0).astype(BF16)
    excl_m = jnp.where(row > col, 1.0, 0.0).astype(BF16)
    parts = _split3(da)
    incl = sum(jnp.dot(incl_m, p, preferred_element_type=F32) for p in parts)
    excl = sum(jnp.dot(excl_m, p, preferred_element_type=F32) for p in parts)
    lane = lax.broadcasted_iota(jnp.int32, da.shape, 1)
    log2dt = jnp.log(dt) * LOG2E
    cum_ref[...] = jnp.where(lane < nfwd, incl, excl)
    src_ref[...] = jnp.where(lane < nfwd, incl - log2dt, excl + log2dt)
    tot_ref[0] = incl[l - 1:l, :]


def _ssd_dt(dt_raw, dt_bias, a_neg):
    t, c = dt_raw.shape
    nc = t // SSD_CHUNK
    row = pl.BlockSpec((1, c), lambda i: (0, 0))
    blk = pl.BlockSpec((SSD_CHUNK, c), lambda i: (i, 0))
    return pl.pallas_call(
        functools.partial(_ssd_dt_kernel, nfwd=c // 2),
        grid=(nc,),
        in_specs=[blk, row, row],
        out_specs=[blk, blk, blk, pl.BlockSpec((1, 1, c), lambda i: (i, 0, 0))],
        out_shape=[jax.ShapeDtypeStruct((t, c), F32)] * 3 + [jax.ShapeDtypeStruct((nc, 1, c), F32)],
        compiler_params=_params("parallel"),
    )(dt_raw, dt_bias.reshape(1, c), a_neg.reshape(1, c))


def _ssd_scan_kernel(xf_ref, bf_ref, cf_ref, pf_ref, ptf_ref, totf_ref, tef_ref,
                     xb_ref, bb_ref, cb_ref, pb_ref, totb_ref, teb_ref,
                     e_ref, dsk_ref, yf_ref, yb_ref, h_ref, *, hpg):
    pdim = SSD_HEAD_DIM
    j = pl.program_id(1)

    @pl.when(j == 0)
    def _():
        h_ref[...] = jnp.zeros_like(h_ref)

    e = e_ref[...]

    def expand(v):
        return jnp.dot(v.astype(BF16), e, preferred_element_type=F32)

    def state_update(d, x32, bm, scale, total_decay):
        xw = (x32 * expand(scale)).astype(BF16)
        inc = lax.dot_general(bm, xw, (((0,), (0,)), ((), ())), preferred_element_type=F32)
        h_ref[d] = h_ref[d] * total_decay + inc

    x = xf_ref[...]
    x32 = x.astype(F32)
    bm = bf_ref[...]
    cm = cf_ref[...]
    p = pf_ref[0]
    pt = ptf_ref[0]
    dtf, cumf = p[:, 0:hpg], p[:, hpg:2 * hpg]
    dtb, cumb = p[:, 2 * hpg:3 * hpg], p[:, 3 * hpg:4 * hpg]
    totf = totf_ref[0, 0][:, 0:hpg]
    l = x.shape[0]
    cb = lax.dot_general(cm, bm, (((1,), (1,)), ((), ())), preferred_element_type=F32)
    li = lax.broadcasted_iota(jnp.int32, (l, l), 0)
    si = lax.broadcasted_iota(jnp.int32, (l, l), 1)
    lower = li >= si
    lane = lax.broadcasted_iota(jnp.int32, (l, 2 * pdim), 1)
    cm32 = cm.astype(F32)
    hf = h_ref[0].astype(BF16)

    def backward_output():
        pb = pb_ref[0]
        totb = totb_ref[0, 0][:, hpg:2 * hpg]
        yb_ref[...] = (jnp.dot(cb_ref[...], h_ref[1].astype(BF16), preferred_element_type=F32)
                       * expand(jnp.exp2(totb - pb[:, 3 * hpg:4 * hpg])))

    def backward_state():
        pb = pb_ref[0]
        state_update(1, xb_ref[...].astype(F32), bb_ref[...],
                     pb[:, 2 * hpg:3 * hpg] * jnp.exp2(pb[:, 3 * hpg:4 * hpg]), teb_ref[0, 0])

    def forward_state():
        state_update(0, x32, bm, dtf * jnp.exp2(totf - cumf), tef_ref[0, 0])

    npair = hpg // 2
    spread = [(npair // 4, backward_output), (npair // 2, backward_state), (3 * npair // 4, forward_state)]
    ys = []
    for q in range(npair):
        rhs = jnp.concatenate([x[:, q * 2 * pdim:(q + 1) * 2 * pdim], hf[:, q * 2 * pdim:(q + 1) * 2 * pdim]],
                              axis=0)
        rs = []
        for h in (2 * q, 2 * q + 1):
            cf_col = jnp.broadcast_to(cumf[:, h:h + 1], (l, l))
            expo = jnp.where(lower, cf_col - pt[h:h + 1, :], pt[hpg + h:hpg + h + 1, :] - cumb[:, h:h + 1])
            lhs = jnp.concatenate([(cb * jnp.exp2(expo)).astype(BF16), (cm32 * jnp.exp2(cf_col)).astype(BF16)],
                                  axis=1)
            rs.append(jnp.dot(lhs, rhs, preferred_element_type=F32))
        ys.append(jnp.where(lane < pdim, rs[0], rs[1]))
        for after, work in spread:
            if after == q:
                work()
    cb_diag = jnp.sum(cm32 * bm.astype(F32), axis=1, keepdims=True)
    yf_ref[...] = jnp.concatenate(ys, axis=1) + (dsk_ref[...] + expand(cb_diag * dtb)) * x32


def _ssd_scan(xbc, dt, cum, src, tot, d_skip, ctx_len):
    t = xbc.shape[0]
    g, n, pdim, lc = SSD_GROUPS, SSD_STATE, SSD_HEAD_DIM, SSD_CHUNK
    heads = dt.shape[1] // 2
    hpg = heads // g
    gw = hpg * pdim
    nc, ncc = t // lc, ctx_len // lc
    assert (g * gw) % n == 0
    b_off, c_off = g * gw // n, g * gw // n + g

    dt4 = dt.reshape(t, 2, g, hpg)
    cum4 = cum.reshape(t, 2, g, hpg)
    pk = jnp.concatenate([dt4[:, 0], cum4[:, 0], dt4[:, 1], cum4[:, 1]], axis=-1)
    pk = jnp.transpose(pk, (1, 0, 2))
    src4 = src.reshape(t, 2, g, hpg)
    pkt = jnp.transpose(jnp.concatenate([src4[:, 0], src4[:, 1]], axis=-1), (1, 2, 0))
    tot4 = tot.reshape(nc, 2, g, hpg)
    totg = jnp.transpose(jnp.concatenate([tot4[:, 0], tot4[:, 1]], axis=-1), (1, 0, 2))
    totg = totg.reshape(g, nc, 1, 2 * hpg)
    tote = jnp.repeat(jnp.exp2(tot4), pdim, axis=-1).reshape(nc, 2, 1, g * gw)
    e = jnp.repeat(jnp.eye(hpg, dtype=BF16), pdim, axis=1)
    dsk = jnp.repeat(d_skip.astype(F32), pdim).reshape(1, g * gw)

    def bwd_chunk(j):
        return jnp.where(j < ncc, ncc - 1 - j, nc - 1 - (j - ncc))

    def fwd_chunk(j):
        return j

    in_specs = []
    for direction, chunk in enumerate((fwd_chunk, bwd_chunk)):
        in_specs += [
            pl.BlockSpec((lc, gw), lambda gi, j, c=chunk: (c(j), gi)),
            pl.BlockSpec((lc, n), lambda gi, j, c=chunk: (c(j), b_off + gi)),
            pl.BlockSpec((lc, n), lambda gi, j, c=chunk: (c(j), c_off + gi)),
            pl.BlockSpec((1, lc, 4 * hpg), lambda gi, j, c=chunk: (gi, c(j), 0)),
        ]
        if direction == 0:
            in_specs.append(pl.BlockSpec((1, 2 * hpg, lc), lambda gi, j: (gi, 0, j)))
        in_specs += [
            pl.BlockSpec((1, 1, 1, 2 * hpg), lambda gi, j, c=chunk: (gi, c(j), 0, 0)),
            pl.BlockSpec((1, 1, 1, gw), lambda gi, j, c=chunk, dr=direction: (c(j), dr, 0, gi)),
        ]
    in_specs += [pl.BlockSpec((hpg, gw), lambda gi, j: (0, 0)),
                 pl.BlockSpec((1, gw), lambda gi, j: (0, gi))]
    out_specs = [pl.BlockSpec((lc, gw), lambda gi, j: (j, gi)),
                 pl.BlockSpec((lc, gw), lambda gi, j: (bwd_chunk(j), gi))]
    return pl.pallas_call(
        functools.partial(_ssd_scan_kernel, hpg=hpg),
        grid=(g, nc),
        in_specs=in_specs,
        out_specs=out_specs,
        out_shape=[jax.ShapeDtypeStruct((t, g * gw), F32)] * 2,
        scratch_shapes=[pltpu.VMEM((2, n, gw), F32)],
        compiler_params=_params("parallel", "arbitrary"),
    )(xbc, xbc, xbc, pk, pkt, totg, tote,
      xbc, xbc, xbc, pk, totg, tote, e, dsk)


def _ssd_finish_kernel(yf_ref, yb_ref, z_ref, w_ref, o_ref):
    y = (yf_ref[...] + yb_ref[...]) * _silu(z_ref[...].astype(F32))
    y = y * lax.rsqrt(jnp.mean(y * y, axis=-1, keepdims=True) + RMS_EPS)
    o_ref[...] = (y * w_ref[...]).astype(o_ref.dtype)


def _ssd_finish(yf, yb, z, norm_w):
    t, c = yf.shape
    gw = c // SSD_GROUPS
    tm = _pick(t, (256, 128))
    blk = pl.BlockSpec((tm, gw), lambda i, gi: (i, gi))
    return pl.pallas_call(
        _ssd_finish_kernel,
        grid=(t // tm, SSD_GROUPS),
        in_specs=[blk, blk, blk, pl.BlockSpec((1, gw), lambda i, gi: (0, gi))],
        out_specs=blk,
        out_shape=jax.ShapeDtypeStruct((t, c), BF16),
        compiler_params=_params("parallel", "parallel"),
    )(yf, yb, z, norm_w.reshape(1, c))


def _ssd_mixer(h, xa, gate, layer, w_in, conv_w, conv_b, a_log, dt_bias, d_skip, norm_w, w_out, ctx_len):
    d_inner = w_out.shape[1]
    conv_ch = conv_w.shape[1]
    n_dt = w_in.shape[2] - d_inner - conv_ch
    z = _matmul(h, w_in, layer=layer, n=d_inner, col_off=0, out_dtype=BF16)
    xbc = _matmul_conv(h, w_in, conv_w, conv_b, layer=layer, n=conv_ch, col_off=d_inner, ctx_len=ctx_len)
    dt_raw = _matmul(h, w_in, layer=layer, n=n_dt, col_off=d_inner + conv_ch, out_dtype=F32)
    a_neg = -jnp.exp(a_log.astype(F32)).reshape(-1)
    dt, cum, src, tot = _ssd_dt(dt_raw, dt_bias.astype(F32).reshape(-1), a_neg)
    yf, yb = _ssd_scan(xbc, dt, cum, src, tot, d_skip, ctx_len)
    yn = _ssd_finish(yf, yb, z, norm_w)
    return _matmul(yn, w_out, layer=layer, mode="res", res=xa, gate=gate, ctx_len=ctx_len)


def _cpow(ar, ai, n):
    rr, ri = None, None
    br, bi = ar, ai
    while n:
        if n & 1:
            if rr is None:
                rr, ri = br, bi
            else:
                rr, ri = rr * br - ri * bi, rr * bi + ri * br
        n >>= 1
        if n:
            br, bi = br * br - bi * bi, 2.0 * br * bi
    return rr, ri


def _s5_kernel(u_ref, wb_ref, wc_ref, a_ref, dsk_ref, o_ref, yacc_ref, sba_ref, sbb_ref, bua_ref, bub_ref, *,
               ctx_len, tt):
    total = u_ref.shape[0]
    half = wb_ref.shape[-1] // 2
    nseg = S5_SEGMENTS
    rowid = lax.broadcasted_iota(jnp.int32, (nseg, half), 0)
    zero = jnp.zeros((nseg, half), F32)
    yacc_ref[...] = jnp.zeros_like(yacc_ref)

    dirs = (0, 1)
    wb = [wb_ref[d, 0] for d in dirs]
    wc = [wc_ref[d, 0] for d in dirs]
    a1 = [(a_ref[d, 0][:, :half], a_ref[d, 0][:, half:]) for d in dirs]
    ab = [(jnp.broadcast_to(r, (nseg, half)), jnp.broadcast_to(i, (nseg, half))) for r, i in a1]
    hcar = [(jnp.zeros((1, half), F32), jnp.zeros((1, half), F32)) for _ in dirs]
    def step(d, bu_ref, t, sr, si):
        ar, ai = ab[d]
        br = bu_ref[d, t * nseg:(t + 1) * nseg, :half]
        bi = bu_ref[d, t * nseg:(t + 1) * nseg, half:]
        return ar * sr - ai * si + br, ar * si + ai * sr + bi

    def pipelined(ntile, project, consume, carry):
        project(0, 0)

        def body(p, c):
            project(2 * p + 1, 1)
            c = consume(2 * p, 0, c)
            project(2 * p + 2, 0)
            return consume(2 * p + 1, 1, c)

        npairs = (ntile - 1) // 2
        carry = lax.fori_loop(0, npairs, body, carry)
        if ntile - 2 * npairs == 2:
            project(ntile - 1, 1)
            carry = consume(ntile - 2, 0, carry)
            return consume(ntile - 1, 1, carry)
        return consume(ntile - 1, 0, carry)

    bu_bufs = (bua_ref, bub_ref)
    s_bufs = (sba_ref, sbb_ref)

    for base, n in ((0, ctx_len), (ctx_len, total - ctx_len)):
        ls = n // nseg
        tts = min(tt, ls)
        tile_rows = nseg * tts
        ntile = ls // tts
        steps = (list(range(tts)), list(reversed(range(tts))))

        def tile_rows_of(d, k, ls=ls, base=base, tts=tts, tile_rows=tile_rows):
            t0 = k * tts if d == 0 else ls - (k + 1) * tts
            start = base + t0 * nseg
            if not isinstance(start, int):
                start = pl.multiple_of(start, math.gcd(base, tile_rows))
            return pl.ds(start, tile_rows)

        def project(k, par, tile_rows_of=tile_rows_of, tile_rows=tile_rows):
            for d in dirs:
                bu_bufs[par][d, 0:tile_rows, :] = jnp.dot(u_ref[tile_rows_of(d, k), :].astype(BF16), wb[d],
                                                          preferred_element_type=F32)

        def pass1(k, par, carry, tts=tts, steps=steps):
            s = list(carry)
            for q in range(tts):
                for d in dirs:
                    s[2 * d], s[2 * d + 1] = step(d, bu_bufs[par], steps[d][q], s[2 * d], s[2 * d + 1])
            return tuple(s)

        sf = pipelined(ntile, project, pass1, (zero,) * 4)

        hin = []
        for d in dirs:
            alr, ali = _cpow(a1[d][0], a1[d][1], ls)
            hr, hi = hcar[d]
            hinr, hini = zero, zero
            for sg in (range(nseg) if d == 0 else reversed(range(nseg))):
                hinr = jnp.where(rowid == sg, hr, hinr)
                hini = jnp.where(rowid == sg, hi, hini)
                hr, hi = (alr * hr - ali * hi + sf[2 * d][sg:sg + 1], alr * hi + ali * hr + sf[2 * d + 1][sg:sg + 1])
            hcar[d] = (hr, hi)
            hin += [hinr, hini]

        def project_out(k, par, tile_rows_of=tile_rows_of, tile_rows=tile_rows):
            for d in dirs:
                y = jnp.dot(s_bufs[par][d, 0:tile_rows, :].astype(BF16), wc[d], preferred_element_type=F32)
                yacc_ref[tile_rows_of(d, k), :] += y

        def pass2(k, par, carry, tts=tts, steps=steps, project_out=project_out):
            project_out(k - 1 if isinstance(k, int) and k > 0 else jnp.maximum(k - 1, 0), 1 - par)
            s = list(carry)
            for q in range(tts):
                for d in dirs:
                    t = steps[d][q]
                    s[2 * d], s[2 * d + 1] = step(d, bu_bufs[par], t, s[2 * d], s[2 * d + 1])
                    s_bufs[par][d, t * nseg:(t + 1) * nseg, :half] = s[2 * d]
                    s_bufs[par][d, t * nseg:(t + 1) * nseg, half:] = s[2 * d + 1]
            return tuple(s)

        s_bufs[1][...] = jnp.zeros_like(s_bufs[1])
        pipelined(ntile, project, pass2, tuple(hin))
        project_out(ntile - 1, (ntile - 1) % 2)

    y = yacc_ref[...] + dsk_ref[...] * u_ref[...]
    o_ref[...] = jax.nn.gelu(y).astype(o_ref.dtype)


def _s5_weights(lam_re, lam_im, log_step, b_re, b_im, c_re, c_im):
    step = jnp.exp(log_step.astype(F32))[..., None]
    lr, li = lam_re.astype(F32), lam_im.astype(F32)
    mag = jnp.exp(lr * step)
    ar, ai = mag * jnp.cos(li * step), mag * jnp.sin(li * step)
    den = lr * lr + li * li
    kr = ((ar - 1.0) * lr + ai * li) / den
    ki = (ai * lr - (ar - 1.0) * li) / den
    br, bi = b_re.astype(F32), b_im.astype(F32)
    bbr = kr[..., None] * br - ki[..., None] * bi
    bbi = kr[..., None] * bi + ki[..., None] * br
    cr, ci = c_re.astype(F32), c_im.astype(F32)
    ng, pst, gc = bbr.shape[1:]
    gpb = LANES // gc
    nb = ng // gpb
    eye = jnp.eye(gpb, dtype=F32)

    def in_w(bb):
        bb = bb.reshape(2, nb, gpb, pst, gc)
        return jnp.einsum('dbgpc,gh->dbgchp', bb, eye).reshape(2, nb, gpb * gc, gpb * pst)

    def out_w(cc):
        cc = cc.reshape(2, nb, gpb, gc, pst)
        return jnp.einsum('dbgcp,gh->dbhpgc', cc, eye).reshape(2, nb, gpb * pst, gpb * gc)

    wb = jnp.concatenate([in_w(bbr), in_w(bbi)], axis=-1).astype(BF16)
    wc = jnp.concatenate([out_w(cr), out_w(-ci)], axis=-2).astype(BF16)
    avec = jnp.concatenate([ar.reshape(2, nb, 1, gpb * pst), ai.reshape(2, nb, 1, gpb * pst)], axis=-1)
    return wb, wc, avec


def _seg_permute(x, ctx_len, inverse):
    def perm(v):
        n = v.shape[0]
        shape = (n // S5_SEGMENTS, S5_SEGMENTS) if inverse else (S5_SEGMENTS, n // S5_SEGMENTS)
        return jnp.swapaxes(v.reshape(*shape, -1), 0, 1).reshape(n, -1)
    return jnp.concatenate([perm(x[:ctx_len]), perm(x[ctx_len:])], axis=0)


def _s5_scan(u, wb, wc, avec, d_skip, ctx_len):
    t, d = u.shape
    nb = d // LANES
    feat = wb.shape[-1]
    tt = S5_TIME_TILE
    for n in (ctx_len, t - ctx_len):
        assert n % S5_SEGMENTS == 0 and (n // S5_SEGMENTS) % min(tt, n // S5_SEGMENTS) == 0
    return pl.pallas_call(
        functools.partial(_s5_kernel, ctx_len=ctx_len, tt=tt),
        grid=(nb,),
        in_specs=[
            pl.BlockSpec((t, LANES), lambda b: (0, b)),
            pl.BlockSpec((2, 1, LANES, feat), lambda b: (0, b, 0, 0)),
            pl.BlockSpec((2, 1, feat, LANES), lambda b: (0, b, 0, 0)),
            pl.BlockSpec((2, 1, 1, feat), lambda b: (0, b, 0, 0)),
            pl.BlockSpec((1, LANES), lambda b: (0, b)),
        ],
        out_specs=pl.BlockSpec((t, LANES), lambda b: (0, b)),
        out_shape=jax.ShapeDtypeStruct((t, d), BF16),
        scratch_shapes=[pltpu.VMEM((t, LANES), F32)] + [pltpu.VMEM((2, S5_SEGMENTS * tt, feat), F32)] * 4,
        compiler_params=_params("parallel"),
    )(u, wb, wc, avec, d_skip.astype(F32).reshape(1, d))


def _s5_mixer(u, xa, gate, layer, lam_re, lam_im, log_step, b_re, b_im, c_re, c_im, d_skip, glu_w, glu_b,
              ctx_len):
    wb, wc, avec = _s5_weights(lam_re, lam_im, log_step, b_re, b_im, c_re, c_im)
    y = _seg_permute(_s5_scan(_seg_permute(u, ctx_len, False), wb, wc, avec, d_skip, ctx_len), ctx_len, True)
    return _matmul(y, glu_w, layer=layer, mode="glu", bias=glu_b.astype(F32), res=xa, gate=gate,
                   ctx_len=ctx_len)


def _qk_prep_kernel(x_ref, g_ref, cos_ref, sin_ref, o_ref, *, dh, scale):
    cos = cos_ref[...]
    sin = sin_ref[...]
    g = g_ref[...]
    lane = lax.broadcasted_iota(jnp.int32, cos.shape, 1)
    quarter = dh // 4
    first = (lane // quarter) % 2 == 0
    for s in range(x_ref.shape[1] // dh):
        x = x_ref[:, s * dh:(s + 1) * dh]
        y = x * lax.rsqrt(jnp.mean(x * x, axis=-1, keepdims=True) + RMS_EPS) * g
        partner = jnp.where(first, pltpu.roll(y, dh - quarter, axis=1), pltpu.roll(y, quarter, axis=1))
        o_ref[:, s * dh:(s + 1) * dh] = ((y * cos + partner * sin) * scale).astype(o_ref.dtype)


def _qk_prep(x, gain, cos, sin, first_block, nblk, scale):
    t = x.shape[0]
    dh = gain.shape[0]
    width = 4 * dh
    tm = _pick(t, (768, 512, 256))
    return pl.pallas_call(
        functools.partial(_qk_prep_kernel, dh=dh, scale=scale),
        grid=(t // tm, nblk),
        in_specs=[
            pl.BlockSpec((tm, width), lambda i, j: (i, j + first_block)),
            pl.BlockSpec((1, dh), lambda i, j: (0, 0)),
            pl.BlockSpec((tm, dh), lambda i, j: (i, 0)),
            pl.BlockSpec((tm, dh), lambda i, j: (i, 0)),
        ],
        out_specs=pl.BlockSpec((tm, width), lambda i, j: (i, j)),
        out_shape=jax.ShapeDtypeStruct((t, nblk * width), BF16),
        compiler_params=_params("parallel", "parallel"),
    )(x, gain.astype(F32).reshape(1, dh), cos, sin)


def _flash_kernel(q_ref, k_ref, v_ref, lam_ref, g_ref, o_ref, m_ref, l_ref, acc_ref, sa_ref, sb_ref, *,
                  ctx_len, tk, dh, post_scale):
    i = pl.program_id(1)
    total = k_ref.shape[0]
    hw = 2 * dh
    nlat = (total - ctx_len) // tk
    m_ref[...] = jnp.full_like(m_ref, -jnp.inf)
    l_ref[...] = jnp.zeros_like(l_ref)
    acc_ref[...] = jnp.zeros_like(acc_ref)

    def lane_groups(x):
        return [x[:, c * LANES:(c + 1) * LANES] for c in range(x.shape[1] // LANES)]

    def scores(j, start, size):
        return lax.dot_general(q_ref[:, j * dh:(j + 1) * dh], k_ref[pl.ds(start, size), j * dh:(j + 1) * dh],
                               (((1,), (1,)), ((), ())), preferred_element_type=F32)

    def absorb(j, s, start, size):
        group_max = functools.reduce(jnp.maximum, lane_groups(s))
        m_old = m_ref[j]
        m_new = jnp.maximum(m_old, jnp.max(group_max, axis=-1, keepdims=True))
        alpha = jnp.exp2(m_old - m_new)
        p = jnp.exp2(s - jnp.tile(m_new, (1, size // LANES)))
        l_ref[j] = alpha * l_ref[j] + functools.reduce(jnp.add, lane_groups(p))
        acc_ref[j] = (jnp.tile(alpha, (1, hw // LANES)) * acc_ref[j]
                      + jnp.dot(p.astype(BF16), v_ref[pl.ds(start, size), :], preferred_element_type=F32))
        m_ref[j] = m_new

    def lat_start(c):
        return pl.multiple_of(ctx_len + c * tk, math.gcd(ctx_len, tk))

    for j in range(2):
        absorb(j, scores(j, 0, ctx_len), 0, ctx_len)

    def stage(c_next, dst_ref, c_cur, src_ref):
        if c_next is not None:
            for j in range(2):
                dst_ref[j] = scores(j, lat_start(c_next), tk)
        if c_cur is not None:
            for j in range(2):
                absorb(j, src_ref[j], lat_start(c_cur), tk)

    @pl.when(i > 0)
    def _():
        npairs = (nlat - 1) // 2
        stage(0, sa_ref, None, None)

        def body(p, carry):
            stage(2 * p + 1, sb_ref, 2 * p, sa_ref)
            stage(2 * p + 2, sa_ref, 2 * p + 1, sb_ref)
            return carry

        lax.fori_loop(0, npairs, body, 0)
        if nlat - 2 * npairs == 2:
            stage(nlat - 1, sb_ref, nlat - 2, sa_ref)
            stage(None, None, nlat - 1, sb_ref)
        else:
            stage(None, None, nlat - 1, sa_ref)

    l0 = jnp.sum(l_ref[0], axis=-1, keepdims=True)
    l1 = jnp.sum(l_ref[1], axis=-1, keepdims=True)
    o = acc_ref[0] / l0 - lam_ref[...] * (acc_ref[1] / l1)
    o = o * lax.rsqrt(jnp.mean(o * o, axis=-1, keepdims=True) + RMS_EPS) * g_ref[...] * post_scale
    o_ref[...] = o.astype(o_ref.dtype)


def _flash(q, k, v, lam, sub_gain, ctx_len, post_scale):
    t, d = q.shape
    dh = sub_gain.shape[0] // 2
    hw = 2 * dh
    tq = _pick(t - ctx_len, (FLASH_Q_ROWS, ctx_len))
    assert tq % ctx_len == 0
    pad = tq - ctx_len
    if pad:
        q = jnp.pad(q, ((pad, 0), (0, 0)))
    tk = _pick(t - ctx_len, (1024, 512, 256, 128))
    lamv = jnp.full((1, hw), lam, F32)
    out = pl.pallas_call(
        functools.partial(_flash_kernel, ctx_len=ctx_len, tk=tk, dh=dh, post_scale=post_scale),
        grid=(d // hw, (t + pad) // tq),
        in_specs=[
            pl.BlockSpec((tq, hw), lambda h, i: (i, h)),
            pl.BlockSpec((t, hw), lambda h, i: (0, h)),
            pl.BlockSpec((t, hw), lambda h, i: (0, h)),
            pl.BlockSpec((1, hw), lambda h, i: (0, 0)),
            pl.BlockSpec((1, hw), lambda h, i: (0, 0)),
        ],
        out_specs=pl.BlockSpec((tq, hw), lambda h, i: (i, h)),
        out_shape=jax.ShapeDtypeStruct((t + pad, d), BF16),
        scratch_shapes=[pltpu.VMEM((2, tq, LANES), F32), pltpu.VMEM((2, tq, LANES), F32),
                        pltpu.VMEM((2, tq, hw), F32), pltpu.VMEM((2, tq, tk), F32), pltpu.VMEM((2, tq, tk), F32)],
        compiler_params=_params("parallel", "arbitrary"),
    )(q, k, v, lamv, sub_gain.astype(F32).reshape(1, hw))
    return out[pad:] if pad else out


def _rope_tables(seq, ctx_len, dh):
    rows = seq // GRID_W
    row = jnp.repeat(jnp.arange(rows, dtype=F32), GRID_W)
    col = jnp.tile(jnp.arange(GRID_W, dtype=F32), rows)
    n_freq = dh // 4
    inv_freq = ROPE_BASE ** (-jnp.arange(n_freq, dtype=F32) / n_freq)
    ang_r = row[:, None] * inv_freq
    ang_c = col[:, None] * inv_freq
    cos = jnp.concatenate([jnp.cos(ang_r), jnp.cos(ang_r), jnp.cos(ang_c), jnp.cos(ang_c)], axis=-1)
    sin = jnp.concatenate([-jnp.sin(ang_r), jnp.sin(ang_r), -jnp.sin(ang_c), jnp.sin(ang_c)], axis=-1)
    cos = jnp.concatenate([jnp.ones((ctx_len, dh), F32), cos], axis=0)
    sin = jnp.concatenate([jnp.zeros((ctx_len, dh), F32), sin], axis=0)
    return cos, sin


def _diff_attn_mixer(h, xa, gate, layer, w_q, w_k, w_v, w_o, q_gain, k_gain, lam_q1, lam_k1, lam_q2, lam_k2,
                     sub_gain, lambda_init, ctx_len):
    t, d = h.shape
    dh = q_gain.shape[0]
    v = _matmul(h, w_v, layer=layer, out_dtype=BF16)
    cos, sin = _rope_tables(t - ctx_len, ctx_len, dh)
    nblk = d // (4 * dh)
    q = _qk_prep(_matmul(h, w_q, layer=layer, out_dtype=F32), q_gain, cos, sin, 0, nblk,
                 dh ** -0.5 * math.log2(math.e))
    k = _qk_prep(_matmul(h, w_k, layer=layer, out_dtype=F32), k_gain, cos, sin, 0, nblk, 1.0)
    lam = (jnp.exp(jnp.sum(lam_q1.astype(F32) * lam_k1.astype(F32)))
           - jnp.exp(jnp.sum(lam_q2.astype(F32) * lam_k2.astype(F32))) + lambda_init)
    o = _flash(q, k, v, lam, sub_gain, ctx_len, 1.0 - lambda_init)
    return _matmul(o, w_o, layer=layer, mode="res", res=xa, gate=gate, ctx_len=ctx_len)


def _conv_ffn(hf, xa, gate, layer, w_up, conv_w, conv_b, w_down, ctx_len):
    tgt = _matmul_conv(hf, w_up, conv_w, conv_b, layer=layer, glu=True, ctx_len=ctx_len)
    return _matmul(tgt, w_down, layer=layer, mode="res", res=xa, gate=gate, ctx_len=ctx_len)


def kernel(x, c, ctx, c_ctx, mod_w, mod_b, norm_mix, norm_ffn, ffn_up, ffn_conv_w, ffn_conv_b, ffn_down, ssd_w_in, ssd_conv_w, ssd_conv_b, ssd_a_log, ssd_dt_bias, ssd_d, ssd_norm, ssd_w_out, s5_lam_re, s5_lam_im, s5_log_step, s5_b_re, s5_b_im, s5_c_re, s5_c_im, s5_d, s5_glu_w, s5_glu_b, da_w_q, da_w_k, da_w_v, da_w_o, da_q_norm, da_k_norm, da_lam_q1, da_lam_k1, da_lam_q2, da_lam_k2, da_sub_norm):
    bsz, seq, d = x.shape
    assert bsz == 1
    ctx_len = ctx.shape[1]
    depth = mod_w.shape[0]
    cond = jnp.concatenate([jax.nn.silu(c.astype(F32)), jax.nn.silu(c_ctx.astype(F32))[None, :]], axis=0)
    mod = _modulation(cond, mod_w, mod_b).reshape(depth, 2, 6, d)
    xa = jnp.concatenate([ctx[0], x[0]], axis=0).astype(F32)

    for i in range(depth):
        kind, j = i % N_MIXERS, i // N_MIXERS
        m = mod[i]
        g1, g2 = m[:, 2], m[:, 5]
        h = _norm_mod(xa, norm_mix[i], m, 0, 1, ctx_len, F32 if kind == 1 else BF16)
        if kind == 0:
            xa = _ssd_mixer(h, xa, g1, j, ssd_w_in, ssd_conv_w[j], ssd_conv_b[j], ssd_a_log[j], ssd_dt_bias[j],
                            ssd_d[j], ssd_norm[j], ssd_w_out, ctx_len)
        elif kind == 1:
            xa = _s5_mixer(h, xa, g1, j, s5_lam_re[j], s5_lam_im[j], s5_log_step[j], s5_b_re[j], s5_b_im[j],
                           s5_c_re[j], s5_c_im[j], s5_d[j], s5_glu_w, s5_glu_b[j], ctx_len)
        else:
            lambda_init = 0.8 - 0.6 * math.exp(-0.3 * i)
            xa = _diff_attn_mixer(h, xa, g1, j, da_w_q, da_w_k, da_w_v, da_w_o, da_q_norm[j],
                                  da_k_norm[j], da_lam_q1[j], da_lam_k1[j], da_lam_q2[j], da_lam_k2[j],
                                  da_sub_norm[j], lambda_init, ctx_len)
        hf = _norm_mod(xa, norm_ffn[i], m, 3, 4, ctx_len, BF16)
        xa = _conv_ffn(hf, xa, g2, i, ffn_up, ffn_conv_w[i], ffn_conv_b[i], ffn_down, ctx_len)
    return xa[ctx_len:][None]
```

```python
import functools
import math

import jax
import jax.numpy as jnp
from jax import lax
from jax.experimental import pallas as pl
from jax.experimental.pallas import tpu as pltpu

F32 = jnp.float32
BF16 = jnp.bfloat16

RMS_EPS = 1e-6
ROPE_BASE = 10000.0
GRID_W = 64
N_MIXERS = 3

LANES = 128
SUBLANES = 8
VMEM_LIMIT_BYTES = 52 * 1024 * 1024

SSD_GROUPS = 8
SSD_HEAD_DIM = 64
SSD_STATE = 128
SSD_CHUNK = 128
S5_SEGMENTS = SUBLANES
S5_TIME_TILE = 64
FLASH_Q_ROWS = 512


def _params(*semantics):
    return pltpu.CompilerParams(dimension_semantics=semantics, vmem_limit_bytes=VMEM_LIMIT_BYTES)


def _pick(n, candidates):
    for c in candidates:
        if n % c == 0:
            return c
    raise ValueError(f"no tile in {candidates} divides {n}")


def _sigmoid(x):
    return 1.0 / (1.0 + jnp.exp(-x))


def _silu(x):
    return x * _sigmoid(x)


def _mod_kernel(c_ref, w_ref, b_ref, o_ref, acc_ref, *, nk):
    k = pl.program_id(2)

    @pl.when(k == 0)
    def _():
        acc_ref[...] = jnp.zeros_like(acc_ref)

    w = w_ref[0]
    tk, tn = w.shape
    for r in range(2):
        cb = jnp.tile(c_ref[r], (1, tn // LANES))
        acc_ref[r] += (w * cb).reshape(tk // SUBLANES, SUBLANES, tn).sum(axis=0)

    @pl.when(k == nk - 1)
    def _():
        o_ref[0] = acc_ref[...].sum(axis=1) + b_ref[0]


def _modulation(cond, mod_w, mod_b):
    depth, d, n = mod_w.shape
    tk = _pick(d, (1024, 512, 256, 128))
    tn = _pick(n, (2048, 1024, 512))
    nk = d // tk
    condb = jnp.broadcast_to(cond[:, :, None], (2, d, LANES))
    return pl.pallas_call(
        functools.partial(_mod_kernel, nk=nk),
        grid=(depth, n // tn, nk),
        in_specs=[
            pl.BlockSpec((2, tk, LANES), lambda l, j, k: (0, k, 0)),
            pl.BlockSpec((1, tk, tn), lambda l, j, k: (l, k, j)),
            pl.BlockSpec((1, 1, tn), lambda l, j, k: (l, 0, j)),
        ],
        out_specs=pl.BlockSpec((1, 2, tn), lambda l, j, k: (l, 0, j)),
        out_shape=jax.ShapeDtypeStruct((depth, 2, n), F32),
        scratch_shapes=[pltpu.VMEM((2, SUBLANES, tn), F32)],
        compiler_params=_params("parallel", "parallel", "arbitrary"),
    )(condb, mod_w, mod_b.reshape(depth, 1, n))


def _norm_mod_kernel(x_ref, g_ref, m_ref, o_ref, *, shift_row, scale_row):
    x = x_ref[...]
    y = x * lax.rsqrt(jnp.mean(x * x, axis=-1, keepdims=True) + RMS_EPS) * g_ref[...]
    m = m_ref[0]
    o_ref[...] = (y * (1.0 + m[scale_row:scale_row + 1]) + m[shift_row:shift_row + 1]).astype(o_ref.dtype)


def _norm_mod(xa, gain, mod, shift_row, scale_row, ctx_len):
    t, d = xa.shape
    tm = _pick(ctx_len, (256, 128))
    ncb = ctx_len // tm
    return pl.pallas_call(
        functools.partial(_norm_mod_kernel, shift_row=shift_row, scale_row=scale_row),
        grid=(t // tm,),
        in_specs=[
            pl.BlockSpec((tm, d), lambda i: (i, 0)),
            pl.BlockSpec((1, d), lambda i: (0, 0)),
            pl.BlockSpec((1, 6, d), lambda i: (jnp.where(i < ncb, 1, 0), 0, 0)),
        ],
        out_specs=pl.BlockSpec((tm, d), lambda i: (i, 0)),
        out_shape=jax.ShapeDtypeStruct((t, d), BF16),
        compiler_params=_params("parallel"),
    )(xa, gain.reshape(1, d), mod)


def _gate_rows(gate_ref, i, tm, tn, ctx_len):
    rows = i * tm + lax.broadcasted_iota(jnp.int32, (tm, tn), 0)
    return jnp.where(rows < ctx_len, gate_ref[1:2, :], gate_ref[0:1, :])


def _mm_kernel(*refs, nk, mode, ctx_len, cast_w):
    i = pl.program_id(1)
    k = pl.program_id(2)
    nw = 2 if mode == "glu" else 1
    a_ref, w_refs, rest = refs[0], refs[1:1 + nw], refs[1 + nw:]
    if mode == "plain":
        o_ref, scratch = rest[0], rest[1:]
    elif mode == "res":
        (res_ref, gate_ref, o_ref), scratch = rest[:3], rest[3:]
    else:
        (ba_ref, bb_ref, res_ref, gate_ref, o_ref), scratch = rest[:5], rest[5:]

    if cast_w:
        wbf, scratch = scratch[:nw], scratch[nw:]

        @pl.when(i == 0)
        def _():
            for dst, src in zip(wbf, w_refs):
                dst[k] = src[...].astype(BF16)

        ws = [r[k] for r in wbf]
    else:
        ws = [r[...] for r in w_refs]
    accs = scratch

    a = a_ref[...]
    prods = [jnp.dot(a, w, preferred_element_type=F32) for w in ws]

    def finish(vals):
        tm, tn = vals[0].shape
        if mode == "plain":
            o_ref[...] = vals[0].astype(o_ref.dtype)
        elif mode == "res":
            o_ref[...] = res_ref[...] + _gate_rows(gate_ref, i, tm, tn, ctx_len) * vals[0]
        else:
            o = (vals[0] + ba_ref[...]) * _sigmoid(vals[1] + bb_ref[...])
            o_ref[...] = res_ref[...] + _gate_rows(gate_ref, i, tm, tn, ctx_len) * o

    if nk == 1:
        finish(prods)
    else:
        @pl.when(k == 0)
        def _():
            for acc, p in zip(accs, prods):
                acc[...] = p

        @pl.when(k > 0)
        def _():
            for acc, p in zip(accs, prods):
                acc[...] += p

        @pl.when(k == nk - 1)
        def _():
            finish([acc[...] for acc in accs])


def _weight_spec(w, layer, tk, tn, col_block, hold_nk=None):
    def kb(i, k):
        return k if hold_nk is None else jnp.where(i == 0, k, hold_nk - 1)

    if w.ndim == 3:
        return pl.BlockSpec((None, tk, tn), lambda j, i, k: (layer, kb(i, k), j + col_block))
    return pl.BlockSpec((tk, tn), lambda j, i, k: (kb(i, k), j + col_block))


def _matmul(a, w, *, layer=0, n=None, col_off=0, out_dtype=F32, mode="plain", res=None, gate=None, bias=None,
            ctx_len=0):
    m, kdim = a.shape
    n_w = w.shape[-1]
    nw = 2 if mode == "glu" else 1
    if n is None:
        n = n_w // nw
    tk = _pick(kdim, (4096, 2048, 1024, 512))
    nk = kdim // tk
    cast_w = w.dtype == F32
    hold = nk if cast_w else None
    tm = _pick(m, (768, 512, 256) if (cast_w and nk > 1) else (1056, 768, 512, 256))
    tn = n if n < LANES else _pick(n, (256, 128) if (cast_w and nw == 2) else (512, 256, 128))
    assert col_off % tn == 0
    nb, cb = n // tn, col_off // tn
    a_spec = pl.BlockSpec((tm, tk), lambda j, i, k: (i, k))
    o_spec = pl.BlockSpec((tm, tn), lambda j, i, k: (i, j))
    w_spec = _weight_spec(w, layer, tk, tn, cb, hold)
    row_spec = pl.BlockSpec((1, tn), lambda j, i, k: (0, j))
    gate_spec = pl.BlockSpec((2, tn), lambda j, i, k: (0, j))
    if mode == "plain":
        in_specs, args = [a_spec, w_spec], (a, w)
    elif mode == "res":
        in_specs, args = [a_spec, w_spec, o_spec, gate_spec], (a, w, res, gate)
    else:
        wb_spec = _weight_spec(w, layer, tk, tn, nb, hold)
        bb_spec = pl.BlockSpec((1, tn), lambda j, i, k: (0, j + nb))
        b2 = bias.reshape(1, n_w)
        in_specs = [a_spec, w_spec, wb_spec, row_spec, bb_spec, o_spec, gate_spec]
        args = (a, w, w, b2, b2, res, gate)
    scratch = [pltpu.VMEM((nk, tk, tn), BF16)] * nw if cast_w else []
    scratch += [pltpu.VMEM((tm, tn), F32)] * nw if nk > 1 else []
    semantics = ("arbitrary",) * 3 if cast_w else ("parallel", "parallel", "arbitrary")
    return pl.pallas_call(
        functools.partial(_mm_kernel, nk=nk, mode=mode, ctx_len=ctx_len, cast_w=cast_w),
        grid=(nb, m // tm, nk),
        in_specs=in_specs,
        out_specs=o_spec,
        out_shape=jax.ShapeDtypeStruct((m, n), out_dtype),
        scratch_shapes=scratch,
        compiler_params=_params(*semantics),
    )(*args)


CONV_HALO = 16


def _mm_conv_kernel(*refs, glu, ctx_len, total, tm, cast_w):
    i = pl.program_id(1)
    nw = 2 if glu else 1
    a_ref, ap_ref, an_ref = refs[:3]
    w_refs = refs[3:3 + nw]
    cw_refs = refs[3 + nw:3 + 2 * nw]
    cb_refs = refs[3 + 2 * nw:3 + 3 * nw]
    o_ref = refs[3 + 3 * nw]
    aext_ref = refs[4 + 3 * nw]
    if cast_w:
        wbf = refs[5 + 3 * nw:5 + 4 * nw]

        @pl.when(i == 0)
        def _():
            for dst, src in zip(wbf, w_refs):
                dst[...] = src[...].astype(BF16)

        w_refs = wbf

    hb = CONV_HALO
    aext_ref[0:hb, :] = ap_ref[...]
    aext_ref[hb:hb + tm, :] = a_ref[...]
    aext_ref[hb + tm:, :] = an_ref[...]
    a = aext_ref[...]
    tn = o_ref.shape[1]
    accs = [jnp.dot(a, w_ref[...], preferred_element_type=F32) for w_ref in w_refs]

    def conv(masked):
        row = i * tm + lax.broadcasted_iota(jnp.int32, (tm, LANES), 0)
        in_ctx = row < ctx_len
        outs = []
        for acc, cw_ref, cb_ref in zip(accs, cw_refs, cb_refs):
            taps = cw_ref.shape[0]
            rad = taps // 2
            out = cb_ref[...] + acc[hb:hb + tm] * cw_ref[rad:rad + 1, :]
            for j in range(taps):
                s = j - rad
                if s == 0:
                    continue
                shifted = pltpu.roll(acc, (-s) % acc.shape[0], axis=0)[hb:hb + tm]
                if masked:
                    src = row + s
                    same_seq = jnp.logical_not(jnp.logical_xor(in_ctx, src < ctx_len))
                    ok = jnp.logical_and(jnp.logical_and(src >= 0, src < total), same_seq)
                    shifted = shifted * jnp.tile(jnp.where(ok, 1.0, 0.0), (1, tn // LANES))
                out = out + shifted * cw_ref[j:j + 1, :]
            outs.append(out)
        o = _silu(outs[0]) * outs[1] if glu else _silu(outs[0])
        o_ref[...] = o.astype(o_ref.dtype)

    start = i * tm
    near_edge = jnp.logical_or(jnp.logical_or(start == 0, start + tm >= total),
                               jnp.logical_and(start < ctx_len + hb, start + tm + hb > ctx_len))
    pl.when(near_edge)(lambda: conv(True))
    pl.when(jnp.logical_not(near_edge))(lambda: conv(False))


def _matmul_conv(a, w, conv_w, conv_b, *, layer=0, n=None, col_off=0, glu=False, ctx_len=0, out_dtype=BF16):
    m, kdim = a.shape
    nw = 2 if glu else 1
    if n is None:
        n = w.shape[-1] // nw
    tm = _pick(m, (768, 512, 256))
    cast_w = w.dtype == F32
    tn = _pick(n, (256, 128) if (cast_w and glu) else (512, 256, 128))
    assert col_off % tn == 0 and tm % CONV_HALO == 0 and m % CONV_HALO == 0
    nb, cb = n // tn, col_off // tn
    hb = CONV_HALO
    rb, nhb = tm // hb, m // hb
    taps = conv_w.shape[0]
    a_spec = pl.BlockSpec((tm, kdim), lambda j, i, k: (i, 0))
    ap_spec = pl.BlockSpec((hb, kdim), lambda j, i, k: (jnp.maximum(i * rb - 1, 0), 0))
    an_spec = pl.BlockSpec((hb, kdim), lambda j, i, k: (jnp.minimum((i + 1) * rb, nhb - 1), 0))
    cb2 = conv_b.astype(F32).reshape(1, -1)
    cw2 = conv_w.astype(F32)
    in_specs = [a_spec, ap_spec, an_spec]
    in_specs += [_weight_spec(w, layer, kdim, tn, cb + h * nb) for h in range(nw)]
    in_specs += [pl.BlockSpec((taps, tn), lambda j, i, k, h=h: (0, j + h * nb)) for h in range(nw)]
    in_specs += [pl.BlockSpec((1, tn), lambda j, i, k, h=h: (0, j + h * nb)) for h in range(nw)]
    scratch = [pltpu.VMEM((tm + 2 * hb, kdim), BF16)]
    scratch += [pltpu.VMEM((kdim, tn), BF16)] * nw if cast_w else []
    return pl.pallas_call(
        functools.partial(_mm_conv_kernel, glu=glu, ctx_len=ctx_len, total=m, tm=tm, cast_w=cast_w),
        grid=(nb, m // tm, 1),
        in_specs=in_specs,
        out_specs=pl.BlockSpec((tm, tn), lambda j, i, k: (i, j)),
        out_shape=jax.ShapeDtypeStruct((m, n), out_dtype),
        scratch_shapes=scratch,
        compiler_params=_params("arbitrary", "arbitrary", "arbitrary"),
    )(a, a, a, *([w] * nw), *([cw2] * nw), *([cb2] * nw))


def _split3(v):
    h1 = v.astype(BF16)
    r1 = v - h1.astype(F32)
    h2 = r1.astype(BF16)
    h3 = (r1 - h2.astype(F32)).astype(BF16)
    return h1, h2, h3


LOG2E = math.log2(math.e)


def _ssd_dt_kernel(raw_ref, bias_ref, a_ref, dt_ref, cum_ref, src_ref, tot_ref, *, nfwd):
    x = raw_ref[...] + bias_ref[...]
    dt = jnp.maximum(x, 0.0) + jnp.log(1.0 + jnp.exp(-jnp.abs(x)))
    dt_ref[...] = dt
    da = dt * (a_ref[...] * LOG2E)
    l = da.shape[0]
    row = lax.broadcasted_iota(jnp.int32, (l, l), 0)
    col = lax.broadcasted_iota(jnp.int32, (l, l), 1)
    incl_m = jnp.where(row >= col, 1.0, 0.0).astype(BF16)
    excl_m = jnp.where(row > col, 1.0, 0.0).astype(BF16)
    parts = _split3(da)
    incl = sum(jnp.dot(incl_m, p, preferred_element_type=F32) for p in parts)
    excl = sum(jnp.dot(excl_m, p, preferred_element_type=F32) for p in parts)
    lane = lax.broadcasted_iota(jnp.int32, da.shape, 1)
    log2dt = jnp.log(dt) * LOG2E
    cum_ref[...] = jnp.where(lane < nfwd, incl, excl)
    src_ref[...] = jnp.where(lane < nfwd, incl - log2dt, excl + log2dt)
    tot_ref[0] = incl[l - 1:l, :]


def _ssd_dt(dt_raw, dt_bias, a_neg):
    t, c = dt_raw.shape
    nc = t // SSD_CHUNK
    row = pl.BlockSpec((1, c), lambda i: (0, 0))
    blk = pl.BlockSpec((SSD_CHUNK, c), lambda i: (i, 0))
    return pl.pallas_call(
        functools.partial(_ssd_dt_kernel, nfwd=c // 2),
        grid=(nc,),
        in_specs=[blk, row, row],
        out_specs=[blk, blk, blk, pl.BlockSpec((1, 1, c), lambda i: (i, 0, 0))],
        out_shape=[jax.ShapeDtypeStruct((t, c), F32)] * 3 + [jax.ShapeDtypeStruct((nc, 1, c), F32)],
        compiler_params=_params("parallel"),
    )(dt_raw, dt_bias.reshape(1, c), a_neg.reshape(1, c))


def _ssd_scan_kernel(xf_ref, bf_ref, cf_ref, pf_ref, ptf_ref, totf_ref, tef_ref,
                     xb_ref, bb_ref, cb_ref, pb_ref, totb_ref, teb_ref,
                     e_ref, dsk_ref, yf_ref, yb_ref, h_ref, *, hpg):
    pdim = SSD_HEAD_DIM
    j = pl.program_id(1)

    @pl.when(j == 0)
    def _():
        h_ref[...] = jnp.zeros_like(h_ref)

    e = e_ref[...]

    def expand(v):
        return jnp.dot(v.astype(BF16), e, preferred_element_type=F32)

    def state_update(d, x32, bm, scale, total_decay):
        xw = (x32 * expand(scale)).astype(BF16)
        inc = lax.dot_general(bm, xw, (((0,), (0,)), ((), ())), preferred_element_type=F32)
        h_ref[d] = h_ref[d] * total_decay + inc

    x = xf_ref[...]
    x32 = x.astype(F32)
    bm = bf_ref[...]
    cm = cf_ref[...]
    p = pf_ref[0]
    pt = ptf_ref[0]
    dtf, cumf = p[:, 0:hpg], p[:, hpg:2 * hpg]
    dtb, cumb = p[:, 2 * hpg:3 * hpg], p[:, 3 * hpg:4 * hpg]
    totf = totf_ref[0, 0][:, 0:hpg]
    l = x.shape[0]
    cb = lax.dot_general(cm, bm, (((1,), (1,)), ((), ())), preferred_element_type=F32)
    li = lax.broadcasted_iota(jnp.int32, (l, l), 0)
    si = lax.broadcasted_iota(jnp.int32, (l, l), 1)
    lower = li >= si
    lane = lax.broadcasted_iota(jnp.int32, (l, 2 * pdim), 1)
    cm32 = cm.astype(F32)
    hf = h_ref[0].astype(BF16)

    def backward_output():
        pb = pb_ref[0]
        totb = totb_ref[0, 0][:, hpg:2 * hpg]
        yb_ref[...] = (jnp.dot(cb_ref[...], h_ref[1].astype(BF16), preferred_element_type=F32)
                       * expand(jnp.exp2(totb - pb[:, 3 * hpg:4 * hpg])))

    def backward_state():
        pb = pb_ref[0]
        state_update(1, xb_ref[...].astype(F32), bb_ref[...],
                     pb[:, 2 * hpg:3 * hpg] * jnp.exp2(pb[:, 3 * hpg:4 * hpg]), teb_ref[0, 0])

    def forward_state():
        state_update(0, x32, bm, dtf * jnp.exp2(totf - cumf), tef_ref[0, 0])

    npair = hpg // 2
    spread = [(npair // 4, backward_output), (npair // 2, backward_state), (3 * npair // 4, forward_state)]
    ys = []
    for q in range(npair):
        rhs = jnp.concatenate([x[:, q * 2 * pdim:(q + 1) * 2 * pdim], hf[:, q * 2 * pdim:(q + 1) * 2 * pdim]],
                              axis=0)
        rs = []
        for h in (2 * q, 2 * q + 1):
            cf_col = jnp.broadcast_to(cumf[:, h:h + 1], (l, l))
            expo = jnp.where(lower, cf_col - pt[h:h + 1, :], pt[hpg + h:hpg + h + 1, :] - cumb[:, h:h + 1])
            lhs = jnp.concatenate([(cb * jnp.exp2(expo)).astype(BF16), (cm32 * jnp.exp2(cf_col)).astype(BF16)],
                                  axis=1)
            rs.append(jnp.dot(lhs, rhs, preferred_element_type=F32))
        ys.append(jnp.where(lane < pdim, rs[0], rs[1]))
        for after, work in spread:
            if after == q:
                work()
    cb_diag = jnp.sum(cm32 * bm.astype(F32), axis=1, keepdims=True)
    yf_ref[...] = jnp.concatenate(ys, axis=1) + (dsk_ref[...] + expand(cb_diag * dtb)) * x32


def _ssd_scan(xbc, dt, cum, src, tot, d_skip, ctx_len):
    t = xbc.shape[0]
    g, n, pdim, lc = SSD_GROUPS, SSD_STATE, SSD_HEAD_DIM, SSD_CHUNK
    heads = dt.shape[1] // 2
    hpg = heads // g
    gw = hpg * pdim
    nc, ncc = t // lc, ctx_len // lc
    assert (g * gw) % n == 0
    b_off, c_off = g * gw // n, g * gw // n + g

    dt4 = dt.reshape(t, 2, g, hpg)
    cum4 = cum.reshape(t, 2, g, hpg)
    pk = jnp.concatenate([dt4[:, 0], cum4[:, 0], dt4[:, 1], cum4[:, 1]], axis=-1)
    pk = jnp.transpose(pk, (1, 0, 2))
    src4 = src.reshape(t, 2, g, hpg)
    pkt = jnp.transpose(jnp.concatenate([src4[:, 0], src4[:, 1]], axis=-1), (1, 2, 0))
    tot4 = tot.reshape(nc, 2, g, hpg)
    totg = jnp.transpose(jnp.concatenate([tot4[:, 0], tot4[:, 1]], axis=-1), (1, 0, 2))
    totg = totg.reshape(g, nc, 1, 2 * hpg)
    tote = jnp.repeat(jnp.exp2(tot4), pdim, axis=-1).reshape(nc, 2, 1, g * gw)
    e = jnp.repeat(jnp.eye(hpg, dtype=BF16), pdim, axis=1)
    dsk = jnp.repeat(d_skip.astype(F32), pdim).reshape(1, g * gw)

    def bwd_chunk(j):
        return jnp.where(j < ncc, ncc - 1 - j, nc - 1 - (j - ncc))

    def fwd_chunk(j):
        return j

    in_specs = []
    for direction, chunk in enumerate((fwd_chunk, bwd_chunk)):
        in_specs += [
            pl.BlockSpec((lc, gw), lambda gi, j, c=chunk: (c(j), gi)),
            pl.BlockSpec((lc, n), lambda gi, j, c=chunk: (c(j), b_off + gi)),
            pl.BlockSpec((lc, n), lambda gi, j, c=chunk: (c(j), c_off + gi)),
            pl.BlockSpec((1, lc, 4 * hpg), lambda gi, j, c=chunk: (gi, c(j), 0)),
        ]
        if direction == 0:
            in_specs.append(pl.BlockSpec((1, 2 * hpg, lc), lambda gi, j: (gi, 0, j)))
        in_specs += [
            pl.BlockSpec((1, 1, 1, 2 * hpg), lambda gi, j, c=chunk: (gi, c(j), 0, 0)),
            pl.BlockSpec((1, 1, 1, gw), lambda gi, j, c=chunk, dr=direction: (c(j), dr, 0, gi)),
        ]
    in_specs += [pl.BlockSpec((hpg, gw), lambda gi, j: (0, 0)),
                 pl.BlockSpec((1, gw), lambda gi, j: (0, gi))]
    out_specs = [pl.BlockSpec((lc, gw), lambda gi, j: (j, gi)),
                 pl.BlockSpec((lc, gw), lambda gi, j: (bwd_chunk(j), gi))]
    return pl.pallas_call(
        functools.partial(_ssd_scan_kernel, hpg=hpg),
        grid=(g, nc),
        in_specs=in_specs,
        out_specs=out_specs,
        out_shape=[jax.ShapeDtypeStruct((t, g * gw), F32)] * 2,
        scratch_shapes=[pltpu.VMEM((2, n, gw), F32)],
        compiler_params=_params("parallel", "arbitrary"),
    )(xbc, xbc, xbc, pk, pkt, totg, tote,
      xbc, xbc, xbc, pk, totg, tote, e, dsk)


def _ssd_finish_kernel(yf_ref, yb_ref, z_ref, w_ref, o_ref):
    y = (yf_ref[...] + yb_ref[...]) * _silu(z_ref[...].astype(F32))
    y = y * lax.rsqrt(jnp.mean(y * y, axis=-1, keepdims=True) + RMS_EPS)
    o_ref[...] = (y * w_ref[...]).astype(o_ref.dtype)


def _ssd_finish(yf, yb, z, norm_w):
    t, c = yf.shape
    gw = c // SSD_GROUPS
    tm = _pick(t, (768, 512, 256, 128))
    blk = pl.BlockSpec((tm, gw), lambda i, gi: (i, gi))
    return pl.pallas_call(
        _ssd_finish_kernel,
        grid=(t // tm, SSD_GROUPS),
        in_specs=[blk, blk, blk, pl.BlockSpec((1, gw), lambda i, gi: (0, gi))],
        out_specs=blk,
        out_shape=jax.ShapeDtypeStruct((t, c), BF16),
        compiler_params=_params("parallel", "parallel"),
    )(yf, yb, z, norm_w.reshape(1, c))


def _ssd_mixer(h, xa, gate, layer, w_in, conv_w, conv_b, a_log, dt_bias, d_skip, norm_w, w_out, ctx_len):
    d_inner = w_out.shape[1]
    conv_ch = conv_w.shape[1]
    n_dt = w_in.shape[2] - d_inner - conv_ch
    z = _matmul(h, w_in, layer=layer, n=d_inner, col_off=0, out_dtype=BF16)
    xbc = _matmul_conv(h, w_in, conv_w, conv_b, layer=layer, n=conv_ch, col_off=d_inner, ctx_len=ctx_len)
    dt_raw = _matmul(h, w_in, layer=layer, n=n_dt, col_off=d_inner + conv_ch, out_dtype=F32)
    a_neg = -jnp.exp(a_log.astype(F32)).reshape(-1)
    dt, cum, src, tot = _ssd_dt(dt_raw, dt_bias.astype(F32).reshape(-1), a_neg)
    yf, yb = _ssd_scan(xbc, dt, cum, src, tot, d_skip, ctx_len)
    yn = _ssd_finish(yf, yb, z, norm_w)
    return _matmul(yn, w_out, layer=layer, mode="res", res=xa, gate=gate, ctx_len=ctx_len)


def _cpow(ar, ai, n):
    rr, ri = None, None
    br, bi = ar, ai
    while n:
        if n & 1:
            if rr is None:
                rr, ri = br, bi
            else:
                rr, ri = rr * br - ri * bi, rr * bi + ri * br
        n >>= 1
        if n:
            br, bi = br * br - bi * bi, 2.0 * br * bi
    return rr, ri


def _s5_kernel(u_ref, wb_ref, wc_ref, a_ref, dsk_ref, o_ref, yacc_ref, sba_ref, sbb_ref, bua_ref, bub_ref, *,
               ctx_len, tt):
    total = u_ref.shape[0]
    half = wb_ref.shape[-1] // 2
    nseg = S5_SEGMENTS
    rowid = lax.broadcasted_iota(jnp.int32, (nseg, half), 0)
    zero = jnp.zeros((nseg, half), F32)
    yacc_ref[...] = jnp.zeros_like(yacc_ref)

    dirs = (0, 1)
    wb = [wb_ref[d, 0] for d in dirs]
    wc = [wc_ref[d, 0] for d in dirs]
    a1 = [(a_ref[d, 0][:, :half], a_ref[d, 0][:, half:]) for d in dirs]
    ab = [(jnp.broadcast_to(r, (nseg, half)), jnp.broadcast_to(i, (nseg, half))) for r, i in a1]
    hcar = [(jnp.zeros((1, half), F32), jnp.zeros((1, half), F32)) for _ in dirs]
    def step(d, bu_ref, t, sr, si):
        ar, ai = ab[d]
        br = bu_ref[d, t * nseg:(t + 1) * nseg, :half]
        bi = bu_ref[d, t * nseg:(t + 1) * nseg, half:]
        return ar * sr - ai * si + br, ar * si + ai * sr + bi

    def pipelined(ntile, project, consume, carry):
        project(0, 0)

        def body(p, c):
            project(2 * p + 1, 1)
            c = consume(2 * p, 0, c)
            project(2 * p + 2, 0)
            return consume(2 * p + 1, 1, c)

        npairs = (ntile - 1) // 2
        carry = lax.fori_loop(0, npairs, body, carry)
        if ntile - 2 * npairs == 2:
            project(ntile - 1, 1)
            carry = consume(ntile - 2, 0, carry)
            return consume(ntile - 1, 1, carry)
        return consume(ntile - 1, 0, carry)

    bu_bufs = (bua_ref, bub_ref)
    s_bufs = (sba_ref, sbb_ref)

    for base, n in ((0, ctx_len), (ctx_len, total - ctx_len)):
        ls = n // nseg
        tts = min(tt, ls)
        tile_rows = nseg * tts
        ntile = ls // tts
        steps = (list(range(tts)), list(reversed(range(tts))))

        def tile_rows_of(d, k, ls=ls, base=base, tts=tts, tile_rows=tile_rows):
            t0 = k * tts if d == 0 else ls - (k + 1) * tts
            start = base + t0 * nseg
            if not isinstance(start, int):
                start = pl.multiple_of(start, math.gcd(base, tile_rows))
            return pl.ds(start, tile_rows)

        def project(k, par, tile_rows_of=tile_rows_of, tile_rows=tile_rows):
            for d in dirs:
                bu_bufs[par][d, 0:tile_rows, :] = jnp.dot(u_ref[tile_rows_of(d, k), :], wb[d],
                                                          preferred_element_type=F32)

        def pass1(k, par, carry, tts=tts, steps=steps):
            s = list(carry)
            for q in range(tts):
                for d in dirs:
                    s[2 * d], s[2 * d + 1] = step(d, bu_bufs[par], steps[d][q], s[2 * d], s[2 * d + 1])
            return tuple(s)

        sf = pipelined(ntile, project, pass1, (zero,) * 4)

        hin = []
        for d in dirs:
            alr, ali = _cpow(a1[d][0], a1[d][1], ls)
            hr, hi = hcar[d]
            hinr, hini = zero, zero
            for sg in (range(nseg) if d == 0 else reversed(range(nseg))):
                hinr = jnp.where(rowid == sg, hr, hinr)
                hini = jnp.where(rowid == sg, hi, hini)
                hr, hi = (alr * hr - ali * hi + sf[2 * d][sg:sg + 1], alr * hi + ali * hr + sf[2 * d + 1][sg:sg + 1])
            hcar[d] = (hr, hi)
            hin += [hinr, hini]

        def project_out(k, par, tile_rows_of=tile_rows_of, tile_rows=tile_rows):
            for d in dirs:
                y = jnp.dot(s_bufs[par][d, 0:tile_rows, :].astype(BF16), wc[d], preferred_element_type=F32)
                yacc_ref[tile_rows_of(d, k), :] += y

        def pass2(k, par, carry, tts=tts, steps=steps, project_out=project_out):
            project_out(k - 1 if isinstance(k, int) and k > 0 else jnp.maximum(k - 1, 0), 1 - par)
            s = list(carry)
            for q in range(tts):
                for d in dirs:
                    t = steps[d][q]
                    s[2 * d], s[2 * d + 1] = step(d, bu_bufs[par], t, s[2 * d], s[2 * d + 1])
                    s_bufs[par][d, t * nseg:(t + 1) * nseg, :half] = s[2 * d]
                    s_bufs[par][d, t * nseg:(t + 1) * nseg, half:] = s[2 * d + 1]
            return tuple(s)

        s_bufs[1][...] = jnp.zeros_like(s_bufs[1])
        pipelined(ntile, project, pass2, tuple(hin))
        project_out(ntile - 1, (ntile - 1) % 2)

    y = yacc_ref[...] + dsk_ref[...] * u_ref[...].astype(F32)
    o_ref[...] = jax.nn.gelu(y).astype(o_ref.dtype)


def _s5_weights(lam_re, lam_im, log_step, b_re, b_im, c_re, c_im):
    step = jnp.exp(log_step.astype(F32))[..., None]
    lr, li = lam_re.astype(F32), lam_im.astype(F32)
    mag = jnp.exp(lr * step)
    ar, ai = mag * jnp.cos(li * step), mag * jnp.sin(li * step)
    den = lr * lr + li * li
    kr = ((ar - 1.0) * lr + ai * li) / den
    ki = (ai * lr - (ar - 1.0) * li) / den
    br, bi = b_re.astype(F32), b_im.astype(F32)
    bbr = kr[..., None] * br - ki[..., None] * bi
    bbi = kr[..., None] * bi + ki[..., None] * br
    cr, ci = c_re.astype(F32), c_im.astype(F32)
    ng, pst, gc = bbr.shape[1:]
    gpb = LANES // gc
    nb = ng // gpb
    eye = jnp.eye(gpb, dtype=F32)

    def in_w(bb):
        bb = bb.reshape(2, nb, gpb, pst, gc)
        return jnp.einsum('dbgpc,gh->dbgchp', bb, eye).reshape(2, nb, gpb * gc, gpb * pst)

    def out_w(cc):
        cc = cc.reshape(2, nb, gpb, gc, pst)
        return jnp.einsum('dbgcp,gh->dbhpgc', cc, eye).reshape(2, nb, gpb * pst, gpb * gc)

    wb = jnp.concatenate([in_w(bbr), in_w(bbi)], axis=-1).astype(BF16)
    wc = jnp.concatenate([out_w(cr), out_w(-ci)], axis=-2).astype(BF16)
    avec = jnp.concatenate([ar.reshape(2, nb, 1, gpb * pst), ai.reshape(2, nb, 1, gpb * pst)], axis=-1)
    return wb, wc, avec


def _seg_permute(x, ctx_len, inverse):
    def perm(v):
        n = v.shape[0]
        shape = (n // S5_SEGMENTS, S5_SEGMENTS) if inverse else (S5_SEGMENTS, n // S5_SEGMENTS)
        return jnp.swapaxes(v.reshape(*shape, -1), 0, 1).reshape(n, -1)
    return jnp.concatenate([perm(x[:ctx_len]), perm(x[ctx_len:])], axis=0)


def _s5_scan(u, wb, wc, avec, d_skip, ctx_len):
    t, d = u.shape
    nb = d // LANES
    feat = wb.shape[-1]
    tt = S5_TIME_TILE
    for n in (ctx_len, t - ctx_len):
        assert n % S5_SEGMENTS == 0 and (n // S5_SEGMENTS) % min(tt, n // S5_SEGMENTS) == 0
    return pl.pallas_call(
        functools.partial(_s5_kernel, ctx_len=ctx_len, tt=tt),
        grid=(nb,),
        in_specs=[
            pl.BlockSpec((t, LANES), lambda b: (0, b)),
            pl.BlockSpec((2, 1, LANES, feat), lambda b: (0, b, 0, 0)),
            pl.BlockSpec((2, 1, feat, LANES), lambda b: (0, b, 0, 0)),
            pl.BlockSpec((2, 1, 1, feat), lambda b: (0, b, 0, 0)),
            pl.BlockSpec((1, LANES), lambda b: (0, b)),
        ],
        out_specs=pl.BlockSpec((t, LANES), lambda b: (0, b)),
        out_shape=jax.ShapeDtypeStruct((t, d), BF16),
        scratch_shapes=[pltpu.VMEM((t, LANES), F32)] + [pltpu.VMEM((2, S5_SEGMENTS * tt, feat), F32)] * 4,
        compiler_params=_params("parallel"),
    )(u, wb, wc, avec, d_skip.astype(F32).reshape(1, d))


def _s5_mixer(u, xa, gate, layer, lam_re, lam_im, log_step, b_re, b_im, c_re, c_im, d_skip, glu_w, glu_b,
              ctx_len):
    wb, wc, avec = _s5_weights(lam_re, lam_im, log_step, b_re, b_im, c_re, c_im)
    y = _seg_permute(_s5_scan(_seg_permute(u, ctx_len, False), wb, wc, avec, d_skip, ctx_len), ctx_len, True)
    return _matmul(y, glu_w, layer=layer, mode="glu", bias=glu_b.astype(F32), res=xa, gate=gate,
                   ctx_len=ctx_len)


def _qk_prep_kernel(x_ref, g_ref, cos_ref, sin_ref, o_ref, *, dh, scale):
    cos = cos_ref[...]
    sin = sin_ref[...]
    g = g_ref[...]
    lane = lax.broadcasted_iota(jnp.int32, cos.shape, 1)
    quarter = dh // 4
    first = (lane // quarter) % 2 == 0
    for s in range(x_ref.shape[1] // dh):
        x = x_ref[:, s * dh:(s + 1) * dh]
        y = x * lax.rsqrt(jnp.mean(x * x, axis=-1, keepdims=True) + RMS_EPS) * g
        partner = jnp.where(first, pltpu.roll(y, dh - quarter, axis=1), pltpu.roll(y, quarter, axis=1))
        o_ref[:, s * dh:(s + 1) * dh] = ((y * cos + partner * sin) * scale).astype(o_ref.dtype)


def _qk_prep(x, gain, cos, sin, first_block, nblk, scale):
    t = x.shape[0]
    dh = gain.shape[0]
    width = 4 * dh
    tm = _pick(t, (768, 512, 256))
    return pl.pallas_call(
        functools.partial(_qk_prep_kernel, dh=dh, scale=scale),
        grid=(t // tm, nblk),
        in_specs=[
            pl.BlockSpec((tm, width), lambda i, j: (i, j + first_block)),
            pl.BlockSpec((1, dh), lambda i, j: (0, 0)),
            pl.BlockSpec((tm, dh), lambda i, j: (i, 0)),
            pl.BlockSpec((tm, dh), lambda i, j: (i, 0)),
        ],
        out_specs=pl.BlockSpec((tm, width), lambda i, j: (i, j)),
        out_shape=jax.ShapeDtypeStruct((t, nblk * width), BF16),
        compiler_params=_params("parallel", "parallel"),
    )(x, gain.astype(F32).reshape(1, dh), cos, sin)


def _flash_kernel(q_ref, k_ref, v_ref, lam_ref, g_ref, o_ref, m_ref, l_ref, acc_ref, sa_ref, sb_ref, *,
                  ctx_len, tk, dh, post_scale):
    i = pl.program_id(1)
    total = k_ref.shape[0]
    hw = 2 * dh
    nlat = (total - ctx_len) // tk
    m_ref[...] = jnp.full_like(m_ref, -jnp.inf)
    l_ref[...] = jnp.zeros_like(l_ref)
    acc_ref[...] = jnp.zeros_like(acc_ref)

    def lane_groups(x):
        return [x[:, c * LANES:(c + 1) * LANES] for c in range(x.shape[1] // LANES)]

    def scores(j, start, size):
        return lax.dot_general(q_ref[:, j * dh:(j + 1) * dh], k_ref[pl.ds(start, size), j * dh:(j + 1) * dh],
                               (((1,), (1,)), ((), ())), preferred_element_type=F32)

    def absorb(j, s, start, size):
        group_max = functools.reduce(jnp.maximum, lane_groups(s))
        m_old = m_ref[j]
        m_new = jnp.maximum(m_old, jnp.max(group_max, axis=-1, keepdims=True))
        alpha = jnp.exp2(m_old - m_new)
        p = jnp.exp2(s - jnp.tile(m_new, (1, size // LANES)))
        l_ref[j] = alpha * l_ref[j] + functools.reduce(jnp.add, lane_groups(p))
        acc_ref[j] = (jnp.tile(alpha, (1, hw // LANES)) * acc_ref[j]
                      + jnp.dot(p.astype(BF16), v_ref[pl.ds(start, size), :], preferred_element_type=F32))
        m_ref[j] = m_new

    def lat_start(c):
        return pl.multiple_of(ctx_len + c * tk, math.gcd(ctx_len, tk))

    for j in range(2):
        absorb(j, scores(j, 0, ctx_len), 0, ctx_len)

    def stage(c_next, dst_ref, c_cur, src_ref):
        if c_next is not None:
            for j in range(2):
                dst_ref[j] = scores(j, lat_start(c_next), tk)
        if c_cur is not None:
            for j in range(2):
                absorb(j, src_ref[j], lat_start(c_cur), tk)

    @pl.when(i > 0)
    def _():
        npairs = (nlat - 1) // 2
        stage(0, sa_ref, None, None)

        def body(p, carry):
            stage(2 * p + 1, sb_ref, 2 * p, sa_ref)
            stage(2 * p + 2, sa_ref, 2 * p + 1, sb_ref)
            return carry

        lax.fori_loop(0, npairs, body, 0)
        if nlat - 2 * npairs == 2:
            stage(nlat - 1, sb_ref, nlat - 2, sa_ref)
            stage(None, None, nlat - 1, sb_ref)
        else:
            stage(None, None, nlat - 1, sa_ref)

    l0 = jnp.sum(l_ref[0], axis=-1, keepdims=True)
    l1 = jnp.sum(l_ref[1], axis=-1, keepdims=True)
    o = acc_ref[0] / l0 - lam_ref[...] * (acc_ref[1] / l1)
    o = o * lax.rsqrt(jnp.mean(o * o, axis=-1, keepdims=True) + RMS_EPS) * g_ref[...] * post_scale
    o_ref[...] = o.astype(o_ref.dtype)


def _flash(q, k, v, lam, sub_gain, ctx_len, post_scale):
    t, d = q.shape
    dh = sub_gain.shape[0] // 2
    hw = 2 * dh
    tq = _pick(t - ctx_len, (FLASH_Q_ROWS, ctx_len))
    assert tq % ctx_len == 0
    pad = tq - ctx_len
    if pad:
        q = jnp.pad(q, ((pad, 0), (0, 0)))
    tk = _pick(t - ctx_len, (1024, 512, 256, 128))
    lamv = jnp.full((1, hw), lam, F32)
    out = pl.pallas_call(
        functools.partial(_flash_kernel, ctx_len=ctx_len, tk=tk, dh=dh, post_scale=post_scale),
        grid=(d // hw, (t + pad) // tq),
        in_specs=[
            pl.BlockSpec((tq, hw), lambda h, i: (i, h)),
            pl.BlockSpec((t, hw), lambda h, i: (0, h)),
            pl.BlockSpec((t, hw), lambda h, i: (0, h)),
            pl.BlockSpec((1, hw), lambda h, i: (0, 0)),
            pl.BlockSpec((1, hw), lambda h, i: (0, 0)),
        ],
        out_specs=pl.BlockSpec((tq, hw), lambda h, i: (i, h)),
        out_shape=jax.ShapeDtypeStruct((t + pad, d), BF16),
        scratch_shapes=[pltpu.VMEM((2, tq, LANES), F32), pltpu.VMEM((2, tq, LANES), F32),
                        pltpu.VMEM((2, tq, hw), F32), pltpu.VMEM((2, tq, tk), F32), pltpu.VMEM((2, tq, tk), F32)],
        compiler_params=_params("parallel", "arbitrary"),
    )(q, k, v, lamv, sub_gain.astype(F32).reshape(1, hw))
    return out[pad:] if pad else out


def _rope_tables(seq, ctx_len, dh):
    rows = seq // GRID_W
    row = jnp.repeat(jnp.arange(rows, dtype=F32), GRID_W)
    col = jnp.tile(jnp.arange(GRID_W, dtype=F32), rows)
    n_freq = dh // 4
    inv_freq = ROPE_BASE ** (-jnp.arange(n_freq, dtype=F32) / n_freq)
    ang_r = row[:, None] * inv_freq
    ang_c = col[:, None] * inv_freq
    cos = jnp.concatenate([jnp.cos(ang_r), jnp.cos(ang_r), jnp.cos(ang_c), jnp.cos(ang_c)], axis=-1)
    sin = jnp.concatenate([-jnp.sin(ang_r), jnp.sin(ang_r), -jnp.sin(ang_c), jnp.sin(ang_c)], axis=-1)
    cos = jnp.concatenate([jnp.ones((ctx_len, dh), F32), cos], axis=0)
    sin = jnp.concatenate([jnp.zeros((ctx_len, dh), F32), sin], axis=0)
    return cos, sin


def _diff_attn_mixer(h, xa, gate, layer, w_q, w_k, w_v, w_o, q_gain, k_gain, lam_q1, lam_k1, lam_q2, lam_k2,
                     sub_gain, lambda_init, ctx_len):
    t, d = h.shape
    dh = q_gain.shape[0]
    v = _matmul(h, w_v, layer=layer, out_dtype=BF16)
    cos, sin = _rope_tables(t - ctx_len, ctx_len, dh)
    nblk = d // (4 * dh)
    q = _qk_prep(_matmul(h, w_q, layer=layer, out_dtype=F32), q_gain, cos, sin, 0, nblk,
                 dh ** -0.5 * math.log2(math.e))
    k = _qk_prep(_matmul(h, w_k, layer=layer, out_dtype=F32), k_gain, cos, sin, 0, nblk, 1.0)
    lam = (jnp.exp(jnp.sum(lam_q1.astype(F32) * lam_k1.astype(F32)))
           - jnp.exp(jnp.sum(lam_q2.astype(F32) * lam_k2.astype(F32))) + lambda_init)
    o = _flash(q, k, v, lam, sub_gain, ctx_len, 1.0 - lambda_init)
    return _matmul(o, w_o, layer=layer, mode="res", res=xa, gate=gate, ctx_len=ctx_len)


def _conv_ffn(hf, xa, gate, layer, w_up, conv_w, conv_b, w_down, ctx_len):
    tgt = _matmul_conv(hf, w_up, conv_w, conv_b, layer=layer, glu=True, ctx_len=ctx_len)
    return _matmul(tgt, w_down, layer=layer, mode="res", res=xa, gate=gate, ctx_len=ctx_len)


def kernel(x, c, ctx, c_ctx, mod_w, mod_b, norm_mix, norm_ffn, ffn_up, ffn_conv_w, ffn_conv_b, ffn_down, ssd_w_in, ssd_conv_w, ssd_conv_b, ssd_a_log, ssd_dt_bias, ssd_d, ssd_norm, ssd_w_out, s5_lam_re, s5_lam_im, s5_log_step, s5_b_re, s5_b_im, s5_c_re, s5_c_im, s5_d, s5_glu_w, s5_glu_b, da_w_q, da_w_k, da_w_v, da_w_o, da_q_norm, da_k_norm, da_lam_q1, da_lam_k1, da_lam_q2, da_lam_k2, da_sub_norm):
    bsz, seq, d = x.shape
    assert bsz == 1
    ctx_len = ctx.shape[1]
    depth = mod_w.shape[0]
    cond = jnp.concatenate([jax.nn.silu(c.astype(F32)), jax.nn.silu(c_ctx.astype(F32))[None, :]], axis=0)
    mod = _modulation(cond, mod_w, mod_b).reshape(depth, 2, 6, d)
    xa = jnp.concatenate([ctx[0], x[0]], axis=0).astype(F32)

    for i in range(depth):
        kind, j = i % N_MIXERS, i // N_MIXERS
        m = mod[i]
        g1, g2 = m[:, 2], m[:, 5]
        h = _norm_mod(xa, norm_mix[i], m, 0, 1, ctx_len)
        if kind == 0:
            xa = _ssd_mixer(h, xa, g1, j, ssd_w_in, ssd_conv_w[j], ssd_conv_b[j], ssd_a_log[j], ssd_dt_bias[j],
                            ssd_d[j], ssd_norm[j], ssd_w_out, ctx_len)
        elif kind == 1:
            xa = _s5_mixer(h, xa, g1, j, s5_lam_re[j], s5_lam_im[j], s5_log_step[j], s5_b_re[j], s5_b_im[j],
                           s5_c_re[j], s5_c_im[j], s5_d[j], s5_glu_w, s5_glu_b[j], ctx_len)
        else:
            lambda_init = 0.8 - 0.6 * math.exp(-0.3 * i)
            xa = _diff_attn_mixer(h, xa, g1, j, da_w_q, da_w_k, da_w_v, da_w_o, da_q_norm[j],
                                  da_k_norm[j], da_lam_q1[j], da_lam_k1[j], da_lam_q2[j], da_lam_k2[j],
                                  da_sub_norm[j], lambda_init, ctx_len)
        hf = _norm_mod(xa, norm_ffn[i], m, 3, 4, ctx_len)
        xa = _conv_ffn(hf, xa, g2, i, ffn_up, ffn_conv_w[i], ffn_conv_b[i], ffn_down, ctx_len)
    return xa[ctx_len:][None]
```

```python
import functools
import math

import jax
import jax.numpy as jnp
from jax import lax
from jax.experimental import pallas as pl
from jax.experimental.pallas import tpu as pltpu

F32 = jnp.float32
BF16 = jnp.bfloat16

RMS_EPS = 1e-6
ROPE_BASE = 10000.0
GRID_W = 64
N_MIXERS = 3

LANES = 128
SUBLANES = 8
VMEM_LIMIT_BYTES = 52 * 1024 * 1024

SSD_GROUPS = 8
SSD_HEAD_DIM = 64
SSD_STATE = 128
SSD_CHUNK = 128
S5_SEGMENTS = SUBLANES
S5_TIME_TILE = 64
FLASH_Q_ROWS = 512


def _params(*semantics):
    return pltpu.CompilerParams(dimension_semantics=semantics, vmem_limit_bytes=VMEM_LIMIT_BYTES)


def _pick(n, candidates):
    for c in candidates:
        if n % c == 0:
            return c
    raise ValueError(f"no tile in {candidates} divides {n}")


def _sigmoid(x):
    return 1.0 / (1.0 + jnp.exp(-x))


def _silu(x):
    return x * _sigmoid(x)


def _mod_kernel(c_ref, w_ref, b_ref, o_ref, acc_ref, *, nk):
    k = pl.program_id(2)

    @pl.when(k == 0)
    def _():
        acc_ref[...] = jnp.zeros_like(acc_ref)

    w = w_ref[0]
    tk, tn = w.shape
    for r in range(2):
        cb = jnp.tile(c_ref[r], (1, tn // LANES))
        acc_ref[r] += (w * cb).reshape(tk // SUBLANES, SUBLANES, tn).sum(axis=0)

    @pl.when(k == nk - 1)
    def _():
        o_ref[0] = acc_ref[...].sum(axis=1) + b_ref[0]


def _modulation(cond, mod_w, mod_b):
    depth, d, n = mod_w.shape
    tk = _pick(d, (1024, 512, 256, 128))
    tn = _pick(n, (2048, 1024, 512))
    nk = d // tk
    condb = jnp.broadcast_to(cond[:, :, None], (2, d, LANES))
    return pl.pallas_call(
        functools.partial(_mod_kernel, nk=nk),
        grid=(depth, n // tn, nk),
        in_specs=[
            pl.BlockSpec((2, tk, LANES), lambda l, j, k: (0, k, 0)),
            pl.BlockSpec((1, tk, tn), lambda l, j, k: (l, k, j)),
            pl.BlockSpec((1, 1, tn), lambda l, j, k: (l, 0, j)),
        ],
        out_specs=pl.BlockSpec((1, 2, tn), lambda l, j, k: (l, 0, j)),
        out_shape=jax.ShapeDtypeStruct((depth, 2, n), F32),
        scratch_shapes=[pltpu.VMEM((2, SUBLANES, tn), F32)],
        compiler_params=_params("parallel", "parallel", "arbitrary"),
    )(condb, mod_w, mod_b.reshape(depth, 1, n))


def _norm_mod_kernel(x_ref, g_ref, m_ref, o_ref, *, shift_row, scale_row):
    x = x_ref[...]
    y = x * lax.rsqrt(jnp.mean(x * x, axis=-1, keepdims=True) + RMS_EPS) * g_ref[...]
    m = m_ref[0]
    o_ref[...] = (y * (1.0 + m[scale_row:scale_row + 1]) + m[shift_row:shift_row + 1]).astype(o_ref.dtype)


def _norm_mod(xa, gain, mod, shift_row, scale_row, ctx_len):
    t, d = xa.shape
    tm = _pick(ctx_len, (256, 128))
    ncb = ctx_len // tm
    return pl.pallas_call(
        functools.partial(_norm_mod_kernel, shift_row=shift_row, scale_row=scale_row),
        grid=(t // tm,),
        in_specs=[
            pl.BlockSpec((tm, d), lambda i: (i, 0)),
            pl.BlockSpec((1, d), lambda i: (0, 0)),
            pl.BlockSpec((1, 6, d), lambda i: (jnp.where(i < ncb, 1, 0), 0, 0)),
        ],
        out_specs=pl.BlockSpec((tm, d), lambda i: (i, 0)),
        out_shape=jax.ShapeDtypeStruct((t, d), BF16),
        compiler_params=_params("parallel"),
    )(xa, gain.reshape(1, d), mod)


def _gate_rows(gate_ref, i, tm, tn, ctx_len):
    rows = i * tm + lax.broadcasted_iota(jnp.int32, (tm, tn), 0)
    return jnp.where(rows < ctx_len, gate_ref[1:2, :], gate_ref[0:1, :])


def _mm_kernel(*refs, nk, mode, ctx_len, cast_w):
    i = pl.program_id(1)
    k = pl.program_id(2)
    nw = 2 if mode == "glu" else 1
    a_ref, w_refs, rest = refs[0], refs[1:1 + nw], refs[1 + nw:]
    if mode == "plain":
        o_ref, scratch = rest[0], rest[1:]
    elif mode == "res":
        (res_ref, gate_ref, o_ref), scratch = rest[:3], rest[3:]
    else:
        (ba_ref, bb_ref, res_ref, gate_ref, o_ref), scratch = rest[:5], rest[5:]

    if cast_w:
        wbf, scratch = scratch[:nw], scratch[nw:]

        @pl.when(i == 0)
        def _():
            for dst, src in zip(wbf, w_refs):
                dst[k] = src[...].astype(BF16)

        ws = [r[k] for r in wbf]
    else:
        ws = [r[...] for r in w_refs]
    accs = scratch

    a = a_ref[...]
    prods = [jnp.dot(a, w, preferred_element_type=F32) for w in ws]

    def finish(vals):
        tm, tn = vals[0].shape
        if mode == "plain":
            o_ref[...] = vals[0].astype(o_ref.dtype)
        elif mode == "res":
            o_ref[...] = res_ref[...] + _gate_rows(gate_ref, i, tm, tn, ctx_len) * vals[0]
        else:
            o = (vals[0] + ba_ref[...]) * _sigmoid(vals[1] + bb_ref[...])
            o_ref[...] = res_ref[...] + _gate_rows(gate_ref, i, tm, tn, ctx_len) * o

    if nk == 1:
        finish(prods)
    else:
        @pl.when(k == 0)
        def _():
            for acc, p in zip(accs, prods):
                acc[...] = p

        @pl.when(k > 0)
        def _():
            for acc, p in zip(accs, prods):
                acc[...] += p

        @pl.when(k == nk - 1)
        def _():
            finish([acc[...] for acc in accs])


def _weight_spec(w, layer, tk, tn, col_block, hold_nk=None):
    def kb(i, k):
        return k if hold_nk is None else jnp.where(i == 0, k, hold_nk - 1)

    if w.ndim == 3:
        return pl.BlockSpec((None, tk, tn), lambda j, i, k: (layer, kb(i, k), j + col_block))
    return pl.BlockSpec((tk, tn), lambda j, i, k: (kb(i, k), j + col_block))


def _matmul(a, w, *, layer=0, n=None, col_off=0, out_dtype=F32, mode="plain", res=None, gate=None, bias=None,
            ctx_len=0):
    m, kdim = a.shape
    n_w = w.shape[-1]
    nw = 2 if mode == "glu" else 1
    if n is None:
        n = n_w // nw
    tk = _pick(kdim, (4096, 2048, 1024, 512))
    nk = kdim // tk
    cast_w = w.dtype == F32
    hold = nk if cast_w else None
    tm = _pick(m, (768, 512, 256) if (cast_w and nk > 1) else (1056, 768, 512, 256))
    tn = n if n < LANES else _pick(n, (256, 128) if (cast_w and nw == 2) else (512, 256, 128))
    assert col_off % tn == 0
    nb, cb = n // tn, col_off // tn
    a_spec = pl.BlockSpec((tm, tk), lambda j, i, k: (i, k))
    o_spec = pl.BlockSpec((tm, tn), lambda j, i, k: (i, j))
    w_spec = _weight_spec(w, layer, tk, tn, cb, hold)
    row_spec = pl.BlockSpec((1, tn), lambda j, i, k: (0, j))
    gate_spec = pl.BlockSpec((2, tn), lambda j, i, k: (0, j))
    if mode == "plain":
        in_specs, args = [a_spec, w_spec], (a, w)
    elif mode == "res":
        in_specs, args = [a_spec, w_spec, o_spec, gate_spec], (a, w, res, gate)
    else:
        wb_spec = _weight_spec(w, layer, tk, tn, nb, hold)
        bb_spec = pl.BlockSpec((1, tn), lambda j, i, k: (0, j + nb))
        b2 = bias.reshape(1, n_w)
        in_specs = [a_spec, w_spec, wb_spec, row_spec, bb_spec, o_spec, gate_spec]
        args = (a, w, w, b2, b2, res, gate)
    scratch = [pltpu.VMEM((nk, tk, tn), BF16)] * nw if cast_w else []
    scratch += [pltpu.VMEM((tm, tn), F32)] * nw if nk > 1 else []
    semantics = ("arbitrary",) * 3 if cast_w else ("parallel", "parallel", "arbitrary")
    return pl.pallas_call(
        functools.partial(_mm_kernel, nk=nk, mode=mode, ctx_len=ctx_len, cast_w=cast_w),
        grid=(nb, m // tm, nk),
        in_specs=in_specs,
        out_specs=o_spec,
        out_shape=jax.ShapeDtypeStruct((m, n), out_dtype),
        scratch_shapes=scratch,
        compiler_params=_params(*semantics),
    )(*args)


CONV_HALO = 16


def _mm_conv_kernel(*refs, glu, ctx_len, total, tm, cast_w):
    i = pl.program_id(1)
    nw = 2 if glu else 1
    a_ref, ap_ref, an_ref = refs[:3]
    w_refs = refs[3:3 + nw]
    cw_refs = refs[3 + nw:3 + 2 * nw]
    cb_refs = refs[3 + 2 * nw:3 + 3 * nw]
    o_ref = refs[3 + 3 * nw]
    aext_ref = refs[4 + 3 * nw]
    if cast_w:
        wbf = refs[5 + 3 * nw:5 + 4 * nw]

        @pl.when(i == 0)
        def _():
            for dst, src in zip(wbf, w_refs):
                dst[...] = src[...].astype(BF16)

        w_refs = wbf

    hb = CONV_HALO
    aext_ref[0:hb, :] = ap_ref[...]
    aext_ref[hb:hb + tm, :] = a_ref[...]
    aext_ref[hb + tm:, :] = an_ref[...]
    a = aext_ref[...]
    tn = o_ref.shape[1]
    accs = [jnp.dot(a, w_ref[...], preferred_element_type=F32) for w_ref in w_refs]

    def conv(masked):
        row = i * tm + lax.broadcasted_iota(jnp.int32, (tm, LANES), 0)
        in_ctx = row < ctx_len
        outs = []
        for acc, cw_ref, cb_ref in zip(accs, cw_refs, cb_refs):
            taps = cw_ref.shape[0]
            rad = taps // 2
            out = cb_ref[...] + acc[hb:hb + tm] * cw_ref[rad:rad + 1, :]
            for j in range(taps):
                s = j - rad
                if s == 0:
                    continue
                shifted = pltpu.roll(acc, (-s) % acc.shape[0], axis=0)[hb:hb + tm]
                if masked:
                    src = row + s
                    same_seq = jnp.logical_not(jnp.logical_xor(in_ctx, src < ctx_len))
                    ok = jnp.logical_and(jnp.logical_and(src >= 0, src < total), same_seq)
                    shifted = shifted * jnp.tile(jnp.where(ok, 1.0, 0.0), (1, tn // LANES))
                out = out + shifted * cw_ref[j:j + 1, :]
            outs.append(out)
        o = _silu(outs[0]) * outs[1] if glu else _silu(outs[0])
        o_ref[...] = o.astype(o_ref.dtype)

    start = i * tm
    near_edge = jnp.logical_or(jnp.logical_or(start == 0, start + tm >= total),
                               jnp.logical_and(start < ctx_len + hb, start + tm + hb > ctx_len))
    pl.when(near_edge)(lambda: conv(True))
    pl.when(jnp.logical_not(near_edge))(lambda: conv(False))


def _matmul_conv(a, w, conv_w, conv_b, *, layer=0, n=None, col_off=0, glu=False, ctx_len=0, out_dtype=BF16):
    m, kdim = a.shape
    nw = 2 if glu else 1
    if n is None:
        n = w.shape[-1] // nw
    tm = _pick(m, (768, 512, 256))
    cast_w = w.dtype == F32
    tn = _pick(n, (256, 128) if (cast_w and glu) else (512, 256, 128))
    assert col_off % tn == 0 and tm % CONV_HALO == 0 and m % CONV_HALO == 0
    nb, cb = n // tn, col_off // tn
    hb = CONV_HALO
    rb, nhb = tm // hb, m // hb
    taps = conv_w.shape[0]
    a_spec = pl.BlockSpec((tm, kdim), lambda j, i, k: (i, 0))
    ap_spec = pl.BlockSpec((hb, kdim), lambda j, i, k: (jnp.maximum(i * rb - 1, 0), 0))
    an_spec = pl.BlockSpec((hb, kdim), lambda j, i, k: (jnp.minimum((i + 1) * rb, nhb - 1), 0))
    cb2 = conv_b.astype(F32).reshape(1, -1)
    cw2 = conv_w.astype(F32)
    in_specs = [a_spec, ap_spec, an_spec]
    in_specs += [_weight_spec(w, layer, kdim, tn, cb + h * nb) for h in range(nw)]
    in_specs += [pl.BlockSpec((taps, tn), lambda j, i, k, h=h: (0, j + h * nb)) for h in range(nw)]
    in_specs += [pl.BlockSpec((1, tn), lambda j, i, k, h=h: (0, j + h * nb)) for h in range(nw)]
    scratch = [pltpu.VMEM((tm + 2 * hb, kdim), BF16)]
    scratch += [pltpu.VMEM((kdim, tn), BF16)] * nw if cast_w else []
    return pl.pallas_call(
        functools.partial(_mm_conv_kernel, glu=glu, ctx_len=ctx_len, total=m, tm=tm, cast_w=cast_w),
        grid=(nb, m // tm, 1),
        in_specs=in_specs,
        out_specs=pl.BlockSpec((tm, tn), lambda j, i, k: (i, j)),
        out_shape=jax.ShapeDtypeStruct((m, n), out_dtype),
        scratch_shapes=scratch,
        compiler_params=_params("arbitrary", "arbitrary", "arbitrary"),
    )(a, a, a, *([w] * nw), *([cw2] * nw), *([cb2] * nw))


def _split3(v):
    h1 = v.astype(BF16)
    r1 = v - h1.astype(F32)
    h2 = r1.astype(BF16)
    h3 = (r1 - h2.astype(F32)).astype(BF16)
    return h1, h2, h3


LOG2E = math.log2(math.e)


def _ssd_dt_kernel(raw_ref, bias_ref, a_ref, dt_ref, cum_ref, src_ref, tot_ref, *, nfwd):
    x = raw_ref[...] + bias_ref[...]
    dt = jnp.maximum(x, 0.0) + jnp.log(1.0 + jnp.exp(-jnp.abs(x)))
    dt_ref[...] = dt
    da = dt * (a_ref[...] * LOG2E)
    l = da.shape[0]
    row = lax.broadcasted_iota(jnp.int32, (l, l), 0)
    col = lax.broadcasted_iota(jnp.int32, (l, l), 1)
    incl_m = jnp.where(row >= col, 1.0, 0.0).astype(BF16)
    excl_m = jnp.where(row > col, 1.0, 0.0).astype(BF16)
    parts = _split3(da)
    incl = sum(jnp.dot(incl_m, p, preferred_element_type=F32) for p in parts)
    excl = sum(jnp.dot(excl_m, p, preferred_element_type=F32) for p in parts)
    lane = lax.broadcasted_iota(jnp.int32, da.shape, 1)
    log2dt = jnp.log(dt) * LOG2E
    cum_ref[...] = jnp.where(lane < nfwd, incl, excl)
    src_ref[...] = jnp.where(lane < nfwd, incl - log2dt, excl + log2dt)
    tot_ref[0] = incl[l - 1:l, :]


def _ssd_dt(dt_raw, dt_bias, a_neg):
    t, c = dt_raw.shape
    nc = t // SSD_CHUNK
    row = pl.BlockSpec((1, c), lambda i: (0, 0))
    blk = pl.BlockSpec((SSD_CHUNK, c), lambda i: (i, 0))
    return pl.pallas_call(
        functools.partial(_ssd_dt_kernel, nfwd=c // 2),
        grid=(nc,),
        in_specs=[blk, row, row],
        out_specs=[blk, blk, blk, pl.BlockSpec((1, 1, c), lambda i: (i, 0, 0))],
        out_shape=[jax.ShapeDtypeStruct((t, c), F32)] * 3 + [jax.ShapeDtypeStruct((nc, 1, c), F32)],
        compiler_params=_params("parallel"),
    )(dt_raw, dt_bias.reshape(1, c), a_neg.reshape(1, c))


def _ssd_scan_kernel(xf_ref, bf_ref, cf_ref, pf_ref, ptf_ref, totf_ref, tef_ref,
                     xb_ref, bb_ref, cb_ref, pb_ref, totb_ref, teb_ref,
                     e_ref, dsk_ref, yf_ref, yb_ref, h_ref, *, hpg):
    pdim = SSD_HEAD_DIM
    j = pl.program_id(1)

    @pl.when(j == 0)
    def _():
        h_ref[...] = jnp.zeros_like(h_ref)

    e = e_ref[...]

    def expand(v):
        return jnp.dot(v.astype(BF16), e, preferred_element_type=F32)

    def state_update(d, x32, bm, scale, total_decay):
        xw = (x32 * expand(scale)).astype(BF16)
        inc = lax.dot_general(bm, xw, (((0,), (0,)), ((), ())), preferred_element_type=F32)
        h_ref[d] = h_ref[d] * total_decay + inc

    x = xf_ref[...]
    x32 = x.astype(F32)
    bm = bf_ref[...]
    cm = cf_ref[...]
    p = pf_ref[0]
    pt = ptf_ref[0]
    dtf, cumf = p[:, 0:hpg], p[:, hpg:2 * hpg]
    dtb, cumb = p[:, 2 * hpg:3 * hpg], p[:, 3 * hpg:4 * hpg]
    totf = totf_ref[0, 0][:, 0:hpg]
    l = x.shape[0]
    cb = lax.dot_general(cm, bm, (((1,), (1,)), ((), ())), preferred_element_type=F32)
    li = lax.broadcasted_iota(jnp.int32, (l, l), 0)
    si = lax.broadcasted_iota(jnp.int32, (l, l), 1)
    lower = li >= si
    lane = lax.broadcasted_iota(jnp.int32, (l, 2 * pdim), 1)
    cm32 = cm.astype(F32)
    hf = h_ref[0].astype(BF16)

    def backward_output():
        pb = pb_ref[0]
        totb = totb_ref[0, 0][:, hpg:2 * hpg]
        yb_ref[...] = (jnp.dot(cb_ref[...], h_ref[1].astype(BF16), preferred_element_type=F32)
                       * expand(jnp.exp2(totb - pb[:, 3 * hpg:4 * hpg]))).astype(yb_ref.dtype)

    def backward_state():
        pb = pb_ref[0]
        state_update(1, xb_ref[...].astype(F32), bb_ref[...],
                     pb[:, 2 * hpg:3 * hpg] * jnp.exp2(pb[:, 3 * hpg:4 * hpg]), teb_ref[0, 0])

    def forward_state():
        state_update(0, x32, bm, dtf * jnp.exp2(totf - cumf), tef_ref[0, 0])

    npair = hpg // 2
    spread = [(npair // 4, backward_output), (npair // 2, backward_state), (3 * npair // 4, forward_state)]
    ys = []
    for q in range(npair):
        rhs = jnp.concatenate([x[:, q * 2 * pdim:(q + 1) * 2 * pdim], hf[:, q * 2 * pdim:(q + 1) * 2 * pdim]],
                              axis=0)
        rs = []
        for h in (2 * q, 2 * q + 1):
            cf_col = jnp.broadcast_to(cumf[:, h:h + 1], (l, l))
            expo = jnp.where(lower, cf_col - pt[h:h + 1, :], pt[hpg + h:hpg + h + 1, :] - cumb[:, h:h + 1])
            lhs = jnp.concatenate([(cb * jnp.exp2(expo)).astype(BF16), (cm32 * jnp.exp2(cf_col)).astype(BF16)],
                                  axis=1)
            rs.append(jnp.dot(lhs, rhs, preferred_element_type=F32))
        ys.append(jnp.where(lane < pdim, rs[0], rs[1]))
        for after, work in spread:
            if after == q:
                work()
    cb_diag = jnp.sum(cm32 * bm.astype(F32), axis=1, keepdims=True)
    yf_ref[...] = (jnp.concatenate(ys, axis=1) + (dsk_ref[...] + expand(cb_diag * dtb)) * x32).astype(yf_ref.dtype)


def _ssd_scan(xbc, dt, cum, src, tot, d_skip, ctx_len):
    t = xbc.shape[0]
    g, n, pdim, lc = SSD_GROUPS, SSD_STATE, SSD_HEAD_DIM, SSD_CHUNK
    heads = dt.shape[1] // 2
    hpg = heads // g
    gw = hpg * pdim
    nc, ncc = t // lc, ctx_len // lc
    assert (g * gw) % n == 0
    b_off, c_off = g * gw // n, g * gw // n + g

    dt4 = dt.reshape(t, 2, g, hpg)
    cum4 = cum.reshape(t, 2, g, hpg)
    pk = jnp.concatenate([dt4[:, 0], cum4[:, 0], dt4[:, 1], cum4[:, 1]], axis=-1)
    pk = jnp.transpose(pk, (1, 0, 2))
    src4 = src.reshape(t, 2, g, hpg)
    pkt = jnp.transpose(jnp.concatenate([src4[:, 0], src4[:, 1]], axis=-1), (1, 2, 0))
    tot4 = tot.reshape(nc, 2, g, hpg)
    totg = jnp.transpose(jnp.concatenate([tot4[:, 0], tot4[:, 1]], axis=-1), (1, 0, 2))
    totg = totg.reshape(g, nc, 1, 2 * hpg)
    tote = jnp.repeat(jnp.exp2(tot4), pdim, axis=-1).reshape(nc, 2, 1, g * gw)
    e = jnp.repeat(jnp.eye(hpg, dtype=BF16), pdim, axis=1)
    dsk = jnp.repeat(d_skip.astype(F32), pdim).reshape(1, g * gw)

    def bwd_chunk(j):
        return jnp.where(j < ncc, ncc - 1 - j, nc - 1 - (j - ncc))

    def fwd_chunk(j):
        return j

    in_specs = []
    for direction, chunk in enumerate((fwd_chunk, bwd_chunk)):
        in_specs += [
            pl.BlockSpec((lc, gw), lambda gi, j, c=chunk: (c(j), gi)),
            pl.BlockSpec((lc, n), lambda gi, j, c=chunk: (c(j), b_off + gi)),
            pl.BlockSpec((lc, n), lambda gi, j, c=chunk: (c(j), c_off + gi)),
            pl.BlockSpec((1, lc, 4 * hpg), lambda gi, j, c=chunk: (gi, c(j), 0)),
        ]
        if direction == 0:
            in_specs.append(pl.BlockSpec((1, 2 * hpg, lc), lambda gi, j: (gi, 0, j)))
        in_specs += [
            pl.BlockSpec((1, 1, 1, 2 * hpg), lambda gi, j, c=chunk: (gi, c(j), 0, 0)),
            pl.BlockSpec((1, 1, 1, gw), lambda gi, j, c=chunk, dr=direction: (c(j), dr, 0, gi)),
        ]
    in_specs += [pl.BlockSpec((hpg, gw), lambda gi, j: (0, 0)),
                 pl.BlockSpec((1, gw), lambda gi, j: (0, gi))]
    out_specs = [pl.BlockSpec((lc, gw), lambda gi, j: (j, gi)),
                 pl.BlockSpec((lc, gw), lambda gi, j: (bwd_chunk(j), gi))]
    return pl.pallas_call(
        functools.partial(_ssd_scan_kernel, hpg=hpg),
        grid=(g, nc),
        in_specs=in_specs,
        out_specs=out_specs,
        out_shape=[jax.ShapeDtypeStruct((t, g * gw), BF16)] * 2,
        scratch_shapes=[pltpu.VMEM((2, n, gw), F32)],
        compiler_params=_params("parallel", "arbitrary"),
    )(xbc, xbc, xbc, pk, pkt, totg, tote,
      xbc, xbc, xbc, pk, totg, tote, e, dsk)


def _ssd_finish_kernel(yf_ref, yb_ref, z_ref, w_ref, o_ref):
    y = (yf_ref[...].astype(F32) + yb_ref[...].astype(F32)) * _silu(z_ref[...].astype(F32))
    y = y * lax.rsqrt(jnp.mean(y * y, axis=-1, keepdims=True) + RMS_EPS)
    o_ref[...] = (y * w_ref[...]).astype(o_ref.dtype)


def _ssd_finish(yf, yb, z, norm_w):
    t, c = yf.shape
    gw = c // SSD_GROUPS
    tm = _pick(t, (768, 512, 256, 128))
    blk = pl.BlockSpec((tm, gw), lambda i, gi: (i, gi))
    return pl.pallas_call(
        _ssd_finish_kernel,
        grid=(t // tm, SSD_GROUPS),
        in_specs=[blk, blk, blk, pl.BlockSpec((1, gw), lambda i, gi: (0, gi))],
        out_specs=blk,
        out_shape=jax.ShapeDtypeStruct((t, c), BF16),
        compiler_params=_params("parallel", "parallel"),
    )(yf, yb, z, norm_w.reshape(1, c))


def _ssd_mixer(h, xa, gate, layer, w_in, conv_w, conv_b, a_log, dt_bias, d_skip, norm_w, w_out, ctx_len):
    d_inner = w_out.shape[1]
    conv_ch = conv_w.shape[1]
    n_dt = w_in.shape[2] - d_inner - conv_ch
    z = _matmul(h, w_in, layer=layer, n=d_inner, col_off=0, out_dtype=BF16)
    xbc = _matmul_conv(h, w_in, conv_w, conv_b, layer=layer, n=conv_ch, col_off=d_inner, ctx_len=ctx_len)
    dt_raw = _matmul(h, w_in, layer=layer, n=n_dt, col_off=d_inner + conv_ch, out_dtype=F32)
    a_neg = -jnp.exp(a_log.astype(F32)).reshape(-1)
    dt, cum, src, tot = _ssd_dt(dt_raw, dt_bias.astype(F32).reshape(-1), a_neg)
    yf, yb = _ssd_scan(xbc, dt, cum, src, tot, d_skip, ctx_len)
    yn = _ssd_finish(yf, yb, z, norm_w)
    return _matmul(yn, w_out, layer=layer, mode="res", res=xa, gate=gate, ctx_len=ctx_len)


def _cpow(ar, ai, n):
    rr, ri = None, None
    br, bi = ar, ai
    while n:
        if n & 1:
            if rr is None:
                rr, ri = br, bi
            else:
                rr, ri = rr * br - ri * bi, rr * bi + ri * br
        n >>= 1
        if n:
            br, bi = br * br - bi * bi, 2.0 * br * bi
    return rr, ri


def _s5_kernel(u_ref, wb_ref, wc_ref, a_ref, dsk_ref, o_ref, yacc_ref, sba_ref, sbb_ref, bua_ref, bub_ref, *,
               ctx_len, tt):
    total = u_ref.shape[0]
    half = wb_ref.shape[-1] // 2
    nseg = S5_SEGMENTS
    rowid = lax.broadcasted_iota(jnp.int32, (nseg, half), 0)
    zero = jnp.zeros((nseg, half), F32)
    yacc_ref[...] = jnp.zeros_like(yacc_ref)

    dirs = (0, 1)
    wb = [wb_ref[d, 0] for d in dirs]
    wc = [wc_ref[d, 0] for d in dirs]
    a1 = [(a_ref[d, 0][:, :half], a_ref[d, 0][:, half:]) for d in dirs]
    ab = [(jnp.broadcast_to(r, (nseg, half)), jnp.broadcast_to(i, (nseg, half))) for r, i in a1]
    hcar = [(jnp.zeros((1, half), F32), jnp.zeros((1, half), F32)) for _ in dirs]
    def step(d, bu_ref, t, sr, si):
        ar, ai = ab[d]
        br = bu_ref[d, t * nseg:(t + 1) * nseg, :half]
        bi = bu_ref[d, t * nseg:(t + 1) * nseg, half:]
        return ar * sr - ai * si + br, ar * si + ai * sr + bi

    def pipelined(ntile, project, consume, carry):
        project(0, 0)

        def body(p, c):
            project(2 * p + 1, 1)
            c = consume(2 * p, 0, c)
            project(2 * p + 2, 0)
            return consume(2 * p + 1, 1, c)

        npairs = (ntile - 1) // 2
        carry = lax.fori_loop(0, npairs, body, carry)
        if ntile - 2 * npairs == 2:
            project(ntile - 1, 1)
            carry = consume(ntile - 2, 0, carry)
            return consume(ntile - 1, 1, carry)
        return consume(ntile - 1, 0, carry)

    bu_bufs = (bua_ref, bub_ref)
    s_bufs = (sba_ref, sbb_ref)

    for base, n in ((0, ctx_len), (ctx_len, total - ctx_len)):
        ls = n // nseg
        tts = min(tt, ls)
        tile_rows = nseg * tts
        ntile = ls // tts
        steps = (list(range(tts)), list(reversed(range(tts))))

        def tile_rows_of(d, k, ls=ls, base=base, tts=tts, tile_rows=tile_rows):
            t0 = k * tts if d == 0 else ls - (k + 1) * tts
            start = base + t0 * nseg
            if not isinstance(start, int):
                start = pl.multiple_of(start, math.gcd(base, tile_rows))
            return pl.ds(start, tile_rows)

        def project(k, par, tile_rows_of=tile_rows_of, tile_rows=tile_rows):
            for d in dirs:
                bu_bufs[par][d, 0:tile_rows, :] = jnp.dot(u_ref[tile_rows_of(d, k), :], wb[d],
                                                          preferred_element_type=F32)

        def pass1(k, par, carry, tts=tts, steps=steps):
            s = list(carry)
            for q in range(tts):
                for d in dirs:
                    s[2 * d], s[2 * d + 1] = step(d, bu_bufs[par], steps[d][q], s[2 * d], s[2 * d + 1])
            return tuple(s)

        sf = pipelined(ntile, project, pass1, (zero,) * 4)

        hin = []
        for d in dirs:
            alr, ali = _cpow(a1[d][0], a1[d][1], ls)
            hr, hi = hcar[d]
            hinr, hini = zero, zero
            for sg in (range(nseg) if d == 0 else reversed(range(nseg))):
                hinr = jnp.where(rowid == sg, hr, hinr)
                hini = jnp.where(rowid == sg, hi, hini)
                hr, hi = (alr * hr - ali * hi + sf[2 * d][sg:sg + 1], alr * hi + ali * hr + sf[2 * d + 1][sg:sg + 1])
            hcar[d] = (hr, hi)
            hin += [hinr, hini]

        def project_out(k, par, tile_rows_of=tile_rows_of, tile_rows=tile_rows):
            for d in dirs:
                y = jnp.dot(s_bufs[par][d, 0:tile_rows, :].astype(BF16), wc[d], preferred_element_type=F32)
                yacc_ref[tile_rows_of(d, k), :] += y

        def pass2(k, par, carry, tts=tts, steps=steps, project_out=project_out):
            project_out(k - 1 if isinstance(k, int) and k > 0 else jnp.maximum(k - 1, 0), 1 - par)
            s = list(carry)
            for q in range(tts):
                for d in dirs:
                    t = steps[d][q]
                    s[2 * d], s[2 * d + 1] = step(d, bu_bufs[par], t, s[2 * d], s[2 * d + 1])
                    s_bufs[par][d, t * nseg:(t + 1) * nseg, :half] = s[2 * d]
                    s_bufs[par][d, t * nseg:(t + 1) * nseg, half:] = s[2 * d + 1]
            return tuple(s)

        s_bufs[1][...] = jnp.zeros_like(s_bufs[1])
        pipelined(ntile, project, pass2, tuple(hin))
        project_out(ntile - 1, (ntile - 1) % 2)

    y = yacc_ref[...] + dsk_ref[...] * u_ref[...].astype(F32)
    o_ref[...] = jax.nn.gelu(y).astype(o_ref.dtype)


def _s5_weights(lam_re, lam_im, log_step, b_re, b_im, c_re, c_im):
    step = jnp.exp(log_step.astype(F32))[..., None]
    lr, li = lam_re.astype(F32), lam_im.astype(F32)
    mag = jnp.exp(lr * step)
    ar, ai = mag * jnp.cos(li * step), mag * jnp.sin(li * step)
    den = lr * lr + li * li
    kr = ((ar - 1.0) * lr + ai * li) / den
    ki = (ai * lr - (ar - 1.0) * li) / den
    br, bi = b_re.astype(F32), b_im.astype(F32)
    bbr = kr[..., None] * br - ki[..., None] * bi
    bbi = kr[..., None] * bi + ki[..., None] * br
    cr, ci = c_re.astype(F32), c_im.astype(F32)
    ng, pst, gc = bbr.shape[1:]
    gpb = LANES // gc
    nb = ng // gpb
    eye = jnp.eye(gpb, dtype=F32)

    def in_w(bb):
        bb = bb.reshape(2, nb, gpb, pst, gc)
        return jnp.einsum('dbgpc,gh->dbgchp', bb, eye).reshape(2, nb, gpb * gc, gpb * pst)

    def out_w(cc):
        cc = cc.reshape(2, nb, gpb, gc, pst)
        return jnp.einsum('dbgcp,gh->dbhpgc', cc, eye).reshape(2, nb, gpb * pst, gpb * gc)

    wb = jnp.concatenate([in_w(bbr), in_w(bbi)], axis=-1).astype(BF16)
    wc = jnp.concatenate([out_w(cr), out_w(-ci)], axis=-2).astype(BF16)
    avec = jnp.concatenate([ar.reshape(2, nb, 1, gpb * pst), ai.reshape(2, nb, 1, gpb * pst)], axis=-1)
    return wb, wc, avec


def _seg_permute(x, ctx_len, inverse):
    def perm(v):
        n = v.shape[0]
        shape = (n // S5_SEGMENTS, S5_SEGMENTS) if inverse else (S5_SEGMENTS, n // S5_SEGMENTS)
        return jnp.swapaxes(v.reshape(*shape, -1), 0, 1).reshape(n, -1)
    return jnp.concatenate([perm(x[:ctx_len]), perm(x[ctx_len:])], axis=0)


def _s5_scan(u, wb, wc, avec, d_skip, ctx_len):
    t, d = u.shape
    nb = d // LANES
    feat = wb.shape[-1]
    tt = S5_TIME_TILE
    for n in (ctx_len, t - ctx_len):
        assert n % S5_SEGMENTS == 0 and (n // S5_SEGMENTS) % min(tt, n // S5_SEGMENTS) == 0
    return pl.pallas_call(
        functools.partial(_s5_kernel, ctx_len=ctx_len, tt=tt),
        grid=(nb,),
        in_specs=[
            pl.BlockSpec((t, LANES), lambda b: (0, b)),
            pl.BlockSpec((2, 1, LANES, feat), lambda b: (0, b, 0, 0)),
            pl.BlockSpec((2, 1, feat, LANES), lambda b: (0, b, 0, 0)),
            pl.BlockSpec((2, 1, 1, feat), lambda b: (0, b, 0, 0)),
            pl.BlockSpec((1, LANES), lambda b: (0, b)),
        ],
        out_specs=pl.BlockSpec((t, LANES), lambda b: (0, b)),
        out_shape=jax.ShapeDtypeStruct((t, d), BF16),
        scratch_shapes=[pltpu.VMEM((t, LANES), F32)] + [pltpu.VMEM((2, S5_SEGMENTS * tt, feat), F32)] * 4,
        compiler_params=_params("parallel"),
    )(u, wb, wc, avec, d_skip.astype(F32).reshape(1, d))


def _s5_mixer(u, xa, gate, layer, lam_re, lam_im, log_step, b_re, b_im, c_re, c_im, d_skip, glu_w, glu_b,
              ctx_len):
    wb, wc, avec = _s5_weights(lam_re, lam_im, log_step, b_re, b_im, c_re, c_im)
    y = _seg_permute(_s5_scan(_seg_permute(u, ctx_len, False), wb, wc, avec, d_skip, ctx_len), ctx_len, True)
    return _matmul(y, glu_w, layer=layer, mode="glu", bias=glu_b.astype(F32), res=xa, gate=gate,
                   ctx_len=ctx_len)


def _qk_prep_kernel(x_ref, g_ref, cos_ref, sin_ref, o_ref, *, dh, scale):
    cos = cos_ref[...]
    sin = sin_ref[...]
    g = g_ref[...]
    lane = lax.broadcasted_iota(jnp.int32, cos.shape, 1)
    quarter = dh // 4
    first = (lane // quarter) % 2 == 0
    for s in range(x_ref.shape[1] // dh):
        x = x_ref[:, s * dh:(s + 1) * dh]
        y = x * lax.rsqrt(jnp.mean(x * x, axis=-1, keepdims=True) + RMS_EPS) * g
        partner = jnp.where(first, pltpu.roll(y, dh - quarter, axis=1), pltpu.roll(y, quarter, axis=1))
        o_ref[:, s * dh:(s + 1) * dh] = ((y * cos + partner * sin) * scale).astype(o_ref.dtype)


def _qk_prep(x, gain, cos, sin, first_block, nblk, scale):
    t = x.shape[0]
    dh = gain.shape[0]
    width = 4 * dh
    tm = _pick(t, (768, 512, 256))
    return pl.pallas_call(
        functools.partial(_qk_prep_kernel, dh=dh, scale=scale),
        grid=(t // tm, nblk),
        in_specs=[
            pl.BlockSpec((tm, width), lambda i, j: (i, j + first_block)),
            pl.BlockSpec((1, dh), lambda i, j: (0, 0)),
            pl.BlockSpec((tm, dh), lambda i, j: (i, 0)),
            pl.BlockSpec((tm, dh), lambda i, j: (i, 0)),
        ],
        out_specs=pl.BlockSpec((tm, width), lambda i, j: (i, j)),
        out_shape=jax.ShapeDtypeStruct((t, nblk * width), BF16),
        compiler_params=_params("parallel", "parallel"),
    )(x, gain.astype(F32).reshape(1, dh), cos, sin)


def _flash_kernel(q_ref, k_ref, v_ref, lam_ref, g_ref, o_ref, m_ref, l_ref, acc_ref, sa_ref, sb_ref, *,
                  ctx_len, tk, dh, post_scale):
    i = pl.program_id(1)
    total = k_ref.shape[0]
    hw = 2 * dh
    nlat = (total - ctx_len) // tk
    m_ref[...] = jnp.full_like(m_ref, -jnp.inf)
    l_ref[...] = jnp.zeros_like(l_ref)
    acc_ref[...] = jnp.zeros_like(acc_ref)

    def lane_groups(x):
        return [x[:, c * LANES:(c + 1) * LANES] for c in range(x.shape[1] // LANES)]

    def scores(j, start, size):
        return lax.dot_general(q_ref[:, j * dh:(j + 1) * dh], k_ref[pl.ds(start, size), j * dh:(j + 1) * dh],
                               (((1,), (1,)), ((), ())), preferred_element_type=F32)

    def absorb(j, s, start, size):
        group_max = functools.reduce(jnp.maximum, lane_groups(s))
        m_old = m_ref[j]
        m_new = jnp.maximum(m_old, jnp.max(group_max, axis=-1, keepdims=True))
        alpha = jnp.exp2(m_old - m_new)
        p = jnp.exp2(s - jnp.tile(m_new, (1, size // LANES)))
        l_ref[j] = alpha * l_ref[j] + functools.reduce(jnp.add, lane_groups(p))
        acc_ref[j] = (jnp.tile(alpha, (1, hw // LANES)) * acc_ref[j]
                      + jnp.dot(p.astype(BF16), v_ref[pl.ds(start, size), :], preferred_element_type=F32))
        m_ref[j] = m_new

    def lat_start(c):
        return pl.multiple_of(ctx_len + c * tk, math.gcd(ctx_len, tk))

    for j in range(2):
        absorb(j, scores(j, 0, ctx_len), 0, ctx_len)

    def stage(c_next, dst_ref, c_cur, src_ref):
        if c_next is not None:
            for j in range(2):
                dst_ref[j] = scores(j, lat_start(c_next), tk)
        if c_cur is not None:
            for j in range(2):
                absorb(j, src_ref[j], lat_start(c_cur), tk)

    @pl.when(i > 0)
    def _():
        npairs = (nlat - 1) // 2
        stage(0, sa_ref, None, None)

        def body(p, carry):
            stage(2 * p + 1, sb_ref, 2 * p, sa_ref)
            stage(2 * p + 2, sa_ref, 2 * p + 1, sb_ref)
            return carry

        lax.fori_loop(0, npairs, body, 0)
        if nlat - 2 * npairs == 2:
            stage(nlat - 1, sb_ref, nlat - 2, sa_ref)
            stage(None, None, nlat - 1, sb_ref)
        else:
            stage(None, None, nlat - 1, sa_ref)

    l0 = jnp.sum(l_ref[0], axis=-1, keepdims=True)
    l1 = jnp.sum(l_ref[1], axis=-1, keepdims=True)
    o = acc_ref[0] / l0 - lam_ref[...] * (acc_ref[1] / l1)
    o = o * lax.rsqrt(jnp.mean(o * o, axis=-1, keepdims=True) + RMS_EPS) * g_ref[...] * post_scale
    o_ref[...] = o.astype(o_ref.dtype)


def _flash(q, k, v, lam, sub_gain, ctx_len, post_scale):
    t, d = q.shape
    dh = sub_gain.shape[0] // 2
    hw = 2 * dh
    tq = _pick(t - ctx_len, (FLASH_Q_ROWS, ctx_len))
    assert tq % ctx_len == 0
    pad = tq - ctx_len
    if pad:
        q = jnp.pad(q, ((pad, 0), (0, 0)))
    tk = _pick(t - ctx_len, (1024, 512, 256, 128))
    lamv = jnp.full((1, hw), lam, F32)
    out = pl.pallas_call(
        functools.partial(_flash_kernel, ctx_len=ctx_len, tk=tk, dh=dh, post_scale=post_scale),
        grid=(d // hw, (t + pad) // tq),
        in_specs=[
            pl.BlockSpec((tq, hw), lambda h, i: (i, h)),
            pl.BlockSpec((t, hw), lambda h, i: (0, h)),
            pl.BlockSpec((t, hw), lambda h, i: (0, h)),
            pl.BlockSpec((1, hw), lambda h, i: (0, 0)),
            pl.BlockSpec((1, hw), lambda h, i: (0, 0)),
        ],
        out_specs=pl.BlockSpec((tq, hw), lambda h, i: (i, h)),
        out_shape=jax.ShapeDtypeStruct((t + pad, d), BF16),
        scratch_shapes=[pltpu.VMEM((2, tq, LANES), F32), pltpu.VMEM((2, tq, LANES), F32),
                        pltpu.VMEM((2, tq, hw), F32), pltpu.VMEM((2, tq, tk), F32), pltpu.VMEM((2, tq, tk), F32)],
        compiler_params=_params("parallel", "arbitrary"),
    )(q, k, v, lamv, sub_gain.astype(F32).reshape(1, hw))
    return out[pad:] if pad else out


def _rope_tables(seq, ctx_len, dh):
    rows = seq // GRID_W
    row = jnp.repeat(jnp.arange(rows, dtype=F32), GRID_W)
    col = jnp.tile(jnp.arange(GRID_W, dtype=F32), rows)
    n_freq = dh // 4
    inv_freq = ROPE_BASE ** (-jnp.arange(n_freq, dtype=F32) / n_freq)
    ang_r = row[:, None] * inv_freq
    ang_c = col[:, None] * inv_freq
    cos = jnp.concatenate([jnp.cos(ang_r), jnp.cos(ang_r), jnp.cos(ang_c), jnp.cos(ang_c)], axis=-1)
    sin = jnp.concatenate([-jnp.sin(ang_r), jnp.sin(ang_r), -jnp.sin(ang_c), jnp.sin(ang_c)], axis=-1)
    cos = jnp.concatenate([jnp.ones((ctx_len, dh), F32), cos], axis=0)
    sin = jnp.concatenate([jnp.zeros((ctx_len, dh), F32), sin], axis=0)
    return cos, sin


def _diff_attn_mixer(h, xa, gate, layer, w_q, w_k, w_v, w_o, q_gain, k_gain, lam_q1, lam_k1, lam_q2, lam_k2,
                     sub_gain, lambda_init, ctx_len):
    t, d = h.shape
    dh = q_gain.shape[0]
    v = _matmul(h, w_v, layer=layer, out_dtype=BF16)
    cos, sin = _rope_tables(t - ctx_len, ctx_len, dh)
    nblk = d // (4 * dh)
    q = _qk_prep(_matmul(h, w_q, layer=layer, out_dtype=F32), q_gain, cos, sin, 0, nblk,
                 dh ** -0.5 * math.log2(math.e))
    k = _qk_prep(_matmul(h, w_k, layer=layer, out_dtype=F32), k_gain, cos, sin, 0, nblk, 1.0)
    lam = (jnp.exp(jnp.sum(lam_q1.astype(F32) * lam_k1.astype(F32)))
           - jnp.exp(jnp.sum(lam_q2.astype(F32) * lam_k2.astype(F32))) + lambda_init)
    o = _flash(q, k, v, lam, sub_gain, ctx_len, 1.0 - lambda_init)
    return _matmul(o, w_o, layer=layer, mode="res", res=xa, gate=gate, ctx_len=ctx_len)


def _conv_ffn(hf, xa, gate, layer, w_up, conv_w, conv_b, w_down, ctx_len):
    tgt = _matmul_conv(hf, w_up, conv_w, conv_b, layer=layer, glu=True, ctx_len=ctx_len)
    return _matmul(tgt, w_down, layer=layer, mode="res", res=xa, gate=gate, ctx_len=ctx_len)


def kernel(x, c, ctx, c_ctx, mod_w, mod_b, norm_mix, norm_ffn, ffn_up, ffn_conv_w, ffn_conv_b, ffn_down, ssd_w_in, ssd_conv_w, ssd_conv_b, ssd_a_log, ssd_dt_bias, ssd_d, ssd_norm, ssd_w_out, s5_lam_re, s5_lam_im, s5_log_step, s5_b_re, s5_b_im, s5_c_re, s5_c_im, s5_d, s5_glu_w, s5_glu_b, da_w_q, da_w_k, da_w_v, da_w_o, da_q_norm, da_k_norm, da_lam_q1, da_lam_k1, da_lam_q2, da_lam_k2, da_sub_norm):
    bsz, seq, d = x.shape
    assert bsz == 1
    ctx_len = ctx.shape[1]
    depth = mod_w.shape[0]
    cond = jnp.concatenate([jax.nn.silu(c.astype(F32)), jax.nn.silu(c_ctx.astype(F32))[None, :]], axis=0)
    mod = _modulation(cond, mod_w, mod_b).reshape(depth, 2, 6, d)
    xa = jnp.concatenate([ctx[0], x[0]], axis=0).astype(F32)

    for i in range(depth):
        kind, j = i % N_MIXERS, i // N_MIXERS
        m = mod[i]
        g1, g2 = m[:, 2], m[:, 5]
        h = _norm_mod(xa, norm_mix[i], m, 0, 1, ctx_len)
        if kind == 0:
            xa = _ssd_mixer(h, xa, g1, j, ssd_w_in, ssd_conv_w[j], ssd_conv_b[j], ssd_a_log[j], ssd_dt_bias[j],
                            ssd_d[j], ssd_norm[j], ssd_w_out, ctx_len)
        elif kind == 1:
            xa = _s5_mixer(h, xa, g1, j, s5_lam_re[j], s5_lam_im[j], s5_log_step[j], s5_b_re[j], s5_b_im[j],
                           s5_c_re[j], s5_c_im[j], s5_d[j], s5_glu_w, s5_glu_b[j], ctx_len)
        else:
            lambda_init = 0.8 - 0.6 * math.exp(-0.3 * i)
            xa = _diff_attn_mixer(h, xa, g1, j, da_w_q, da_w_k, da_w_v, da_w_o, da_q_norm[j],
                                  da_k_norm[j], da_lam_q1[j], da_lam_k1[j], da_lam_q2[j], da_lam_k2[j],
                                  da_sub_norm[j], lambda_init, ctx_len)
        hf = _norm_mod(xa, norm_ffn[i], m, 3, 4, ctx_len)
        xa = _conv_ffn(hf, xa, g2, i, ffn_up, ffn_conv_w[i], ffn_conv_b[i], ffn_down, ctx_len)
    return xa[ctx_len:][None]
```
